```python
import jax, jax.numpy as jnp
from jax import lax
import numpy as np

D_MODEL = 1024
BATCH = 4
SEQ = 4096
DEPTH = 4
DEC_BATCH = 128
DEC_SEQ = 1
PAST_LEN = 2048
PAGE_SIZE = 128

HEAD_DIM = 64
A_WIDTH = D_MODEL // 2
A_HEADS = A_WIDTH // HEAD_DIM
A_KV_HEADS = A_HEADS // 4
KV_WIDTH = A_KV_HEADS * HEAD_DIM
CMP_BLOCK = 32
CMP_STRIDE = CMP_BLOCK // 2
SEL_BLOCK = 64
SEL_TOP = 16
WINDOW = 512
FORCE_SCORE = 1.0e4
ROT_DIM = HEAD_DIM // 4
ROPE_THETA = 500000.0
B_WIDTH = D_MODEL // 4
CONV_W = 31
C_WIDTH = D_MODEL // 4
C_HEADS = C_WIDTH // HEAD_DIM
LORA_W = 32
LORA_A = 32
LORA_G = 64
RW_SPLITS = (C_WIDTH, C_WIDTH, C_WIDTH, LORA_W, LORA_A, LORA_G)
RW_COLS = sum(RW_SPLITS)
IN_SPLITS = (A_WIDTH,) + (KV_WIDTH,) * 6 + (3 * A_HEADS, B_WIDTH, B_WIDTH, RW_COLS)
IN_COLS = sum(IN_SPLITS)
D_FF = 11 * D_MODEL // 4
RMS_EPS = 1e-6
LN_EPS = 1e-5
GN_EPS = 64e-5

kernel_name = 'nsa_conv_rwkv7_hybrid_step'


def split_cols(x, sizes):
    cuts = [int(c) for c in np.cumsum(sizes)[:-1]]
    return jnp.split(x, cuts, axis=-1)


def rms_norm(x, g):
    xf = x.astype(jnp.float32)
    y = xf * lax.rsqrt(jnp.mean(xf * xf, axis=-1, keepdims=True) + RMS_EPS)
    return (y * g.astype(jnp.float32)).astype(x.dtype)


def layer_norm(x, g, b, eps):
    xf = x.astype(jnp.float32)
    mu = jnp.mean(xf, axis=-1, keepdims=True)
    var = jnp.mean(jnp.square(xf - mu), axis=-1, keepdims=True)
    return (xf - mu) * lax.rsqrt(var + eps) * g.astype(jnp.float32) + b.astype(jnp.float32)


def swiglu(x, wg, wu, wd):
    return (jax.nn.silu(x @ wg) * (x @ wu)) @ wd


def masked_softmax(s, mask):
    s = jnp.where(mask, s.astype(jnp.float32), -1e30)
    m = jnp.max(s, axis=-1, keepdims=True)
    e = jnp.where(mask, jnp.exp(s - m), 0.0)
    d = jnp.sum(e, axis=-1, keepdims=True)
    return e / jnp.maximum(d, 1e-30)


def rope_partial(x, pos):
    half = ROT_DIM // 2
    inv = ROPE_THETA ** (-jnp.arange(half, dtype=jnp.float32) * 2.0 / ROT_DIM)
    ang = pos.astype(jnp.float32)[:, None] * inv[None, :]
    cos = jnp.cos(ang)[None, :, None, :]
    sin = jnp.sin(ang)[None, :, None, :]
    x1 = x[..., :half].astype(jnp.float32)
    x2 = x[..., half:ROT_DIM].astype(jnp.float32)
    out = jnp.concatenate([x1 * cos - x2 * sin, x2 * cos + x1 * sin, x[..., ROT_DIM:].astype(jnp.float32)], axis=-1)
    return out.astype(x.dtype)


def compress(x, pe, w):
    B, L, G, Dh = x.shape
    n_cmp = (L - CMP_BLOCK) // CMP_STRIDE + 1
    c = x[:, :(n_cmp + 1) * CMP_STRIDE].reshape(B, n_cmp + 1, CMP_STRIDE, G, Dh)
    blk = jnp.concatenate([c[:, :-1], c[:, 1:]], axis=2) + pe[:, None, :]
    return jnp.einsum('bnlgd,lde->bnge', blk, w.reshape(CMP_BLOCK, Dh, Dh))


def sel_blocks(x):
    B, L, G, Dh = x.shape
    n_sel = -(-L // SEL_BLOCK)
    x = jnp.pad(x, ((0, 0), (0, n_sel * SEL_BLOCK - L), (0, 0), (0, 0)))
    return x.reshape(B, n_sel, SEL_BLOCK, G, Dh).transpose(0, 3, 1, 2, 4).reshape(B, G, n_sel, SEL_BLOCK * Dh)


def nsa_attention(q, q_rot, k_cmp, v_cmp, k_sel, v_sel, k_win, v_win, gates, pos0, win_pos0, cmp_pe, cmp_w):
    B, T, H, Dh = q.shape
    L, G = k_cmp.shape[1], k_cmp.shape[2]
    HPG = H // G
    scale = Dh ** -0.5
    q_c5 = q.reshape(B, T, G, HPG, Dh)
    q_r5 = q_rot.reshape(B, T, G, HPG, Dh)
    kc = compress(k_cmp, cmp_pe[0], cmp_w[0])
    vc = compress(v_cmp, cmp_pe[1], cmp_w[1])
    n_cmp = kc.shape[1]
    cmp_end = jnp.arange(n_cmp) * CMP_STRIDE + CMP_BLOCK - 1
    n_sel = -(-L // SEL_BLOCK)
    n_top = min(SEL_TOP, n_sel)
    ci = jnp.arange(n_cmp)[:, None] * CMP_STRIDE
    sj = jnp.arange(n_sel)[None, :] * SEL_BLOCK
    overlap = ((ci < sj + SEL_BLOCK) & (ci + CMP_BLOCK > sj)).astype(jnp.float32)
    ksb = sel_blocks(k_sel)
    vsb = sel_blocks(v_sel)
    kw_pad = jnp.pad(k_win, ((0, 0), (WINDOW, 0), (0, 0), (0, 0)))
    vw_pad = jnp.pad(v_win, ((0, 0), (WINDOW, 0), (0, 0), (0, 0)))
    QB = 128 if T % 128 == 0 else T
    blk_id = jnp.arange(n_sel)
    bi = jnp.arange(B)[:, None, None]
    gi = jnp.arange(G)[None, :, None]

    def block(i):
        lo = i * QB
        p0 = pos0 + lo
        t = p0 + jnp.arange(QB)
        qc = lax.dynamic_slice_in_dim(q_c5, lo, QB, 1)
        qr = lax.dynamic_slice_in_dim(q_r5, lo, QB, 1)
        g = jax.nn.sigmoid(lax.dynamic_slice_in_dim(gates, lo, QB, 1).astype(jnp.float32))
        s = jnp.einsum('bqghd,bngd->bghqn', qc, kc) * scale
        p_cmp = masked_softmax(s, cmp_end[None, :] <= t[:, None])
        o_cmp = jnp.einsum('bghqn,bngd->bqghd', p_cmp.astype(vc.dtype), vc)
        imp = jnp.einsum('bghqn,nj->bgqj', p_cmp, overlap)
        cur = t // SEL_BLOCK
        valid = blk_id[None, :] <= cur[:, None]
        forced = (blk_id[None, :] == 0) | (blk_id[None, :] == cur[:, None]) | (blk_id[None, :] == cur[:, None] - 1)
        score = jnp.where(valid, jnp.where(forced, FORCE_SCORE, imp), -1.0)
        _, idx = lax.top_k(score, n_top)
        flat = idx.reshape(B, G, QB * n_top)
        kg = ksb[bi, gi, flat].reshape(B, G, QB, n_top, SEL_BLOCK, Dh)
        vg = vsb[bi, gi, flat].reshape(B, G, QB, n_top, SEL_BLOCK, Dh)
        kpos = idx[..., None] * SEL_BLOCK + jnp.arange(SEL_BLOCK)
        m_sel = (kpos <= t[None, None, :, None, None]).reshape(B, G, 1, QB, n_top * SEL_BLOCK)
        s = jnp.einsum('bqghd,bgqkld->bghqkl', qr, kg) * scale
        p = masked_softmax(s.reshape(B, G, HPG, QB, n_top * SEL_BLOCK), m_sel).reshape(s.shape)
        o_sel = jnp.einsum('bghqkl,bgqkld->bqghd', p.astype(vg.dtype), vg)
        kwb = lax.dynamic_slice_in_dim(kw_pad, p0 - win_pos0, WINDOW + QB, 1)
        vwb = lax.dynamic_slice_in_dim(vw_pad, p0 - win_pos0, WINDOW + QB, 1)
        kp = p0 - WINDOW + jnp.arange(WINDOW + QB)
        m_win = (kp[None, :] >= win_pos0) & (kp[None, :] >= t[:, None] - WINDOW) & (kp[None, :] <= t[:, None])
        s = jnp.einsum('bqghd,bkgd->bghqk', qr, kwb) * scale
        p = masked_softmax(s, m_win)
        o_win = jnp.einsum('bghqk,bkgd->bqghd', p.astype(vwb.dtype), vwb)
        o = (g[..., 0:1] * o_cmp.reshape(B, QB, H, Dh) + g[..., 1:2] * o_sel.reshape(B, QB, H, Dh)
             + g[..., 2:3] * o_win.reshape(B, QB, H, Dh))
        return o.reshape(B, QB, H * Dh).astype(q.dtype)

    out = lax.map(block, jnp.arange(T // QB))
    return out.transpose(1, 0, 2, 3).reshape(B, T, H * Dh)


def conv_mixer(cv, cg, buf, p):
    u = cv * jax.nn.sigmoid(cg)
    u_all = jnp.concatenate([buf, u], axis=1)
    c = lax.conv_general_dilated(u_all, p['conv_w'][:, None, :], (1,), 'VALID',
                                 dimension_numbers=('NWC', 'WIO', 'NWC'),
                                 feature_group_count=B_WIDTH) + p['conv_b']
    c = jax.nn.silu(layer_norm(c, p['conv_ln_g'], p['conv_ln_b'], LN_EPS)).astype(u.dtype)
    return c @ p['conv_pw'] + p['conv_pw_b'], u_all[:, -(CONV_W - 1):]


def wkv7_step(S, inp):
    r, w, k, v, kk, a = inp
    sk = jnp.einsum('bhvk,bhk->bhv', S, -kk)
    S = S * w[:, :, None, :] + sk[..., None] * (kk * a)[:, :, None, :] + v[..., None] * k[:, :, None, :]
    return S, jnp.einsum('bhvk,bhk->bhv', S, r)


def rwkv7_mixer(rw, shift_prev, s0, p):
    B, T, _ = rw.shape
    f32 = jnp.float32
    prev = jnp.concatenate([shift_prev[:, None], rw[:, :-1]], axis=1)
    xm = rw + (prev - rw) * p['rw_mu']
    r, k, v, wd, ad, gd = split_cols(xm, RW_SPLITS)
    w_log = -jax.nn.softplus(-(p['rw_w0'] + jnp.tanh(wd) @ p['rw_w2']).astype(f32)) - 0.5
    decay = jnp.exp(-jnp.exp(w_log))
    a = jax.nn.sigmoid((p['rw_a0'] + ad @ p['rw_a2']).astype(f32))
    gate = (jax.nn.sigmoid(gd) @ p['rw_g2']).astype(f32)
    hd = lambda z: z.astype(f32).reshape(B, T, C_HEADS, HEAD_DIM)
    hv = lambda z: z.astype(f32).reshape(C_HEADS, HEAD_DIM)
    r, k, v, decay, a = hd(r), hd(k), hd(v), hd(decay), hd(a)
    kk = k * hv(p['rw_kk'])
    kk = kk / jnp.maximum(jnp.sqrt(jnp.sum(kk * kk, axis=-1, keepdims=True)), 1e-12)
    k = k * (1.0 + (a - 1.0) * hv(p['rw_ka']))
    xs = tuple(z.transpose(1, 0, 2, 3) for z in (r, decay, k, v, kk, a))
    s_fin, o = lax.scan(wkv7_step, s0.astype(f32), xs)
    o = o.transpose(1, 0, 2, 3)
    o = layer_norm(o, hv(p['rw_ln_w']), hv(p['rw_ln_b']), GN_EPS)
    o = o + jnp.sum(r * k * p['rw_rk'].astype(f32), axis=-1, keepdims=True) * v
    o = o.reshape(B, T, C_WIDTH) * gate
    return o.astype(rw.dtype), rw[:, -1], s_fin


def token_mix(xn, pos0, past, p):
    B, T, _ = xn.shape
    q, kc, vc, ks, vs, kw, vw, gt, cv, cg, rw = split_cols(xn @ p['w_in'], IN_SPLITS)
    pos = pos0 + jnp.arange(T)
    kv = lambda z: z.reshape(B, T, A_KV_HEADS, HEAD_DIM)
    q = q.reshape(B, T, A_HEADS, HEAD_DIM)
    q_rot = rope_partial(q, pos)
    kc, vc, vs, vw = kv(kc), kv(vc), kv(vs), kv(vw)
    ks, kw = rope_partial(kv(ks), pos), rope_partial(kv(kw), pos)
    if past is None:
        ctx = (kc, vc, ks, vs)
        kw_all, vw_all = kw, vw
        n_buf = min(WINDOW, T)
        conv_buf = jnp.zeros((B, CONV_W - 1, B_WIDTH), xn.dtype)
        shift_prev = jnp.zeros((B, RW_COLS), xn.dtype)
        s0 = jnp.zeros((B, C_HEADS, HEAD_DIM, HEAD_DIM), jnp.float32)
    else:
        ctx = (jnp.concatenate([past['cmp_k'], kc], axis=1), jnp.concatenate([past['cmp_v'], vc], axis=1),
               jnp.concatenate([past['sel_k'], ks], axis=1), jnp.concatenate([past['sel_v'], vs], axis=1))
        kw_all = jnp.concatenate([past['win_k'], kw], axis=1)
        vw_all = jnp.concatenate([past['win_v'], vw], axis=1)
        n_buf = past['win_k'].shape[1]
        conv_buf, shift_prev, s0 = past['conv'], past['shift'], past['wkv']
    win_pos0 = pos0 + T - kw_all.shape[1]
    o_a = nsa_attention(q, q_rot, ctx[0], ctx[1], ctx[2], ctx[3], kw_all, vw_all,
                        gt.reshape(B, T, A_HEADS, 3), pos0, win_pos0, p['cmp_pe'], p['cmp_w'])
    o_b, conv_new = conv_mixer(cv, cg, conv_buf, p)
    o_c, shift_new, s_new = rwkv7_mixer(rw, shift_prev, s0, p)
    gn = p['grp_norm']
    merged = jnp.concatenate([rms_norm(o_a, gn[:A_WIDTH]),
                              rms_norm(o_b, gn[A_WIDTH:A_WIDTH + B_WIDTH]),
                              rms_norm(o_c, gn[A_WIDTH + B_WIDTH:])], axis=-1)
    new = dict(cmp_k=kc, cmp_v=vc, sel_k=ks, sel_v=vs,
               win_k=kw_all[:, -n_buf:], win_v=vw_all[:, -n_buf:],
               conv=conv_new, shift=shift_new, wkv=s_new.astype(xn.dtype))
    return merged @ p['w_out'], new


def trunk(x, pos0, past_fn, W):
    h = x
    states = []
    for l in range(DEPTH):
        p = {n: a[l] for n, a in W.items()}
        nr = p['norms']
        h = h + 0.5 * rms_norm(swiglu(rms_norm(h, nr[0]), p['ffn_gate'][0], p['ffn_up'][0], p['ffn_down'][0]), nr[1])
        m, st = token_mix(rms_norm(h, nr[2]), pos0, None if past_fn is None else past_fn(l), p)
        h = h + rms_norm(m, nr[3])
        h = h + 0.5 * rms_norm(swiglu(rms_norm(h, nr[4]), p['ffn_gate'][1], p['ffn_up'][1], p['ffn_down'][1]), nr[5])
        states.append(st)
    stacked = {k: jnp.stack([s[k] for s in states]) for k in states[0]}
    return h, stacked


def setup_inputs(seed: int = 0) -> dict:
    key = jax.random.key(seed)
    ks = iter(jax.random.split(key, 48))
    nrm = lambda shape, s: jax.random.normal(next(ks), shape, jnp.float32) * s
    gain = lambda shape: 1.0 + nrm(shape, 0.05)
    n_pages = PAST_LEN // PAGE_SIZE
    n_used = DEC_BATCH * n_pages
    n_pool = n_used + max(1, n_used // 4)
    win_buf = min(WINDOW, PAST_LEN)
    pool = (DEPTH, n_pool, PAGE_SIZE, A_KV_HEADS, HEAD_DIM)
    page_table = jax.random.permutation(next(ks), n_pool)[:n_used].reshape(DEC_BATCH, n_pages).astype(jnp.int32)
    return {
        'x_prompt': nrm((BATCH, SEQ, D_MODEL), 1.0),
        'x_sample': nrm((DEC_BATCH, DEC_SEQ, D_MODEL), 1.0),
        'cache_cmp_k': nrm(pool, 1.0),
        'cache_cmp_v': nrm(pool, 1.0),
        'cache_sel_k': nrm(pool, 1.0),
        'cache_sel_v': nrm(pool, 1.0),
        'cache_win_k': nrm((DEPTH, DEC_BATCH, win_buf, A_KV_HEADS, HEAD_DIM), 1.0),
        'cache_win_v': nrm((DEPTH, DEC_BATCH, win_buf, A_KV_HEADS, HEAD_DIM), 1.0),
        'state_conv': nrm((DEPTH, DEC_BATCH, CONV_W - 1, B_WIDTH), 0.5),
        'state_shift': nrm((DEPTH, DEC_BATCH, RW_COLS), 1.0),
        'state_wkv': nrm((DEPTH, DEC_BATCH, C_HEADS, HEAD_DIM, HEAD_DIM), 0.1),
        'page_table': page_table,
        'w_in': nrm((DEPTH, D_MODEL, IN_COLS), D_MODEL ** -0.5),
        'w_out': nrm((DEPTH, D_MODEL, D_MODEL), D_MODEL ** -0.5),
        'norms': gain((DEPTH, 6, D_MODEL)),
        'grp_norm': gain((DEPTH, D_MODEL)),
        'ffn_gate': nrm((DEPTH, 2, D_MODEL, D_FF), D_MODEL ** -0.5),
        'ffn_up': nrm((DEPTH, 2, D_MODEL, D_FF), D_MODEL ** -0.5),
        'ffn_down': nrm((DEPTH, 2, D_FF, D_MODEL), D_FF ** -0.5),
        'cmp_pe': nrm((DEPTH, 2, CMP_BLOCK, HEAD_DIM), 0.5),
        'cmp_w': nrm((DEPTH, 2, CMP_BLOCK * HEAD_DIM, HEAD_DIM), (CMP_BLOCK * HEAD_DIM) ** -0.5),
        'conv_w': nrm((DEPTH, CONV_W, B_WIDTH), CONV_W ** -0.5),
        'conv_b': nrm((DEPTH, B_WIDTH), 0.01),
        'conv_ln_g': gain((DEPTH, B_WIDTH)),
        'conv_ln_b': nrm((DEPTH, B_WIDTH), 0.01),
        'conv_pw': nrm((DEPTH, B_WIDTH, B_WIDTH), B_WIDTH ** -0.5),
        'conv_pw_b': nrm((DEPTH, B_WIDTH), 0.01),
        'rw_mu': jax.random.uniform(next(ks), (DEPTH, RW_COLS), jnp.float32),
        'rw_w0': jax.random.uniform(next(ks), (DEPTH, C_WIDTH), jnp.float32, -2.0, 1.0),
        'rw_w2': nrm((DEPTH, LORA_W, C_WIDTH), 0.1),
        'rw_a0': nrm((DEPTH, C_WIDTH), 0.5),
        'rw_a2': nrm((DEPTH, LORA_A, C_WIDTH), 0.1),
        'rw_g2': nrm((DEPTH, LORA_G, C_WIDTH), LORA_G ** -0.5),
        'rw_kk': 0.85 + nrm((DEPTH, C_WIDTH), 0.1),
        'rw_ka': gain((DEPTH, C_WIDTH)),
        'rw_rk': nrm((DEPTH, C_HEADS, HEAD_DIM), 0.1),
        'rw_ln_w': gain((DEPTH, C_WIDTH)),
        'rw_ln_b': nrm((DEPTH, C_WIDTH), 0.01),
    }


def reference(x_prompt, x_sample, cache_cmp_k, cache_cmp_v, cache_sel_k, cache_sel_v, cache_win_k, cache_win_v,
              state_conv, state_shift, state_wkv, page_table, w_in, w_out, norms, grp_norm, ffn_gate, ffn_up,
              ffn_down, cmp_pe, cmp_w, conv_w, conv_b, conv_ln_g, conv_ln_b, conv_pw, conv_pw_b, rw_mu, rw_w0,
              rw_w2, rw_a0, rw_a2, rw_g2, rw_kk, rw_ka, rw_rk, rw_ln_w, rw_ln_b):
    W = dict(w_in=w_in, w_out=w_out, norms=norms, grp_norm=grp_norm, ffn_gate=ffn_gate, ffn_up=ffn_up,
             ffn_down=ffn_down, cmp_pe=cmp_pe, cmp_w=cmp_w, conv_w=conv_w, conv_b=conv_b, conv_ln_g=conv_ln_g,
             conv_ln_b=conv_ln_b, conv_pw=conv_pw, conv_pw_b=conv_pw_b, rw_mu=rw_mu, rw_w0=rw_w0, rw_w2=rw_w2,
             rw_a0=rw_a0, rw_a2=rw_a2, rw_g2=rw_g2, rw_kk=rw_kk, rw_ka=rw_ka, rw_rk=rw_rk, rw_ln_w=rw_ln_w,
             rw_ln_b=rw_ln_b)
    y_prompt, sp = trunk(x_prompt, 0, None, W)
    past_len = page_table.shape[1] * cache_cmp_k.shape[2]
    n_dec = x_sample.shape[0]

    def past_fn(l):
        def paged(c):
            return c[l][page_table].reshape(n_dec, past_len, A_KV_HEADS, HEAD_DIM)
        return dict(cmp_k=paged(cache_cmp_k), cmp_v=paged(cache_cmp_v),
                    sel_k=paged(cache_sel_k), sel_v=paged(cache_sel_v),
                    win_k=cache_win_k[l], win_v=cache_win_v[l],
                    conv=state_conv[l], shift=state_shift[l], wkv=state_wkv[l])

    y_sample, ss = trunk(x_sample, past_len, past_fn, W)
    return (y_prompt, y_sample,
            sp['cmp_k'], sp['cmp_v'], sp['sel_k'], sp['sel_v'], sp['win_k'], sp['win_v'],
            sp['conv'], sp['shift'], sp['wkv'],
            ss['cmp_k'], ss['cmp_v'], ss['sel_k'], ss['sel_v'], ss['win_k'], ss['win_v'],
            ss['conv'], ss['shift'], ss['wkv'])
```

```python
import functools

import numpy as np
import jax
import jax.numpy as jnp
from jax import lax
from jax.experimental import pallas as pl
from jax.experimental.pallas import tpu as pltpu

F32 = jnp.float32
BF16 = jnp.bfloat16

D_MODEL = 1024
HEAD_DIM = 64
A_WIDTH = 512
A_HEADS = 8
A_KV_HEADS = 2
KV_WIDTH = 128
HPG = A_HEADS // A_KV_HEADS
CMP_BLOCK = 32
CMP_STRIDE = 16
SEL_BLOCK = 64
SEL_TOP = 16
WINDOW = 512
FORCE_SCORE = 1.0e4
ROT_DIM = 16
ROPE_THETA = 500000.0
B_WIDTH = 256
CONV_W = 31
C_WIDTH = 256
C_HEADS = 4
LORA_W = 32
LORA_A = 32
LORA_G = 64
RW_COLS = 3 * C_WIDTH + LORA_W + LORA_A + LORA_G
D_FF = 2816
RMS_EPS = 1e-6
LN_EPS = 1e-5
GN_EPS = 64e-5

LANE = 128
QB = 128
KT = 512
FF_CHUNK = 1408
IN_PAD = 2816
NEG = -1e30
VMEM_BIG = 56 * 1024 * 1024


def _dot(a, b):
    return jnp.dot(a, b, preferred_element_type=F32)


def _dot_nt(a, b):
    return lax.dot_general(a, b, (((1,), (1,)), ((), ())), preferred_element_type=F32)


def _rms(x, g):
    return x * lax.rsqrt(jnp.mean(x * x, axis=-1, keepdims=True) + RMS_EPS) * g


def _split(x):
    hi = x.astype(BF16)
    lo = (x - hi.astype(F32)).astype(BF16)
    return hi, lo


def _dot2(x, w):
    hi, lo = _split(x)
    return _dot(hi, w) + _dot(lo, w)


def _full(shape):
    nd = len(shape)
    return pl.BlockSpec(shape, lambda *_: (0,) * nd)


def _params(sem, vmem=None):
    return pltpu.CompilerParams(dimension_semantics=sem, vmem_limit_bytes=vmem)


def _row_tile(n, cap):
    t = cap
    while n % t:
        t //= 2
    return t


def _ffn_body(h_ref, npre_ref, npost_ref, wg_ref, wu_ref, wd_ref, o_ref):
    h = h_ref[...]
    xn = _rms(h, npre_ref[...]).astype(BF16)
    acc = jnp.zeros(h.shape, F32)
    for c in range(D_FF // FF_CHUNK):
        sl = slice(c * FF_CHUNK, (c + 1) * FF_CHUNK)
        g = _dot(xn, wg_ref[:, sl])
        u = _dot(xn, wu_ref[:, sl])
        a = (g * jax.nn.sigmoid(g) * u).astype(BF16)
        acc = acc + _dot(a, wd_ref[sl, :])
    o_ref[...] = h + 0.5 * _rms(acc, npost_ref[...])


def _ffn(h, npre, npost, wg, wu, wd):
    n = h.shape[0]
    tm = _row_tile(n, 512)
    row = pl.BlockSpec((tm, D_MODEL), lambda i: (i, 0))
    return pl.pallas_call(
        _ffn_body, grid=(n // tm,),
        in_specs=[row, _full((1, D_MODEL)), _full((1, D_MODEL)), _full((D_MODEL, D_FF)),
                  _full((D_MODEL, D_FF)), _full((D_FF, D_MODEL))],
        out_specs=row, out_shape=jax.ShapeDtypeStruct((n, D_MODEL), F32),
        compiler_params=_params(("parallel",), VMEM_BIG), name="ffn")(h, npre, npost, wg, wu, wd)


def _proj_body(h_ref, n_ref, w_ref, rc_ref, rsa_ref, rsb_ref, yq_ref, kc_ref, vc_ref, ks_ref, vs_ref,
               kw_ref, vw_ref, gt_ref, cvcg_ref, rw_ref):
    xn = _rms(h_ref[...], n_ref[...]).astype(BF16)
    c, sa, sb = rc_ref[...], rsa_ref[...], rsb_ref[...]

    def rope(x):
        return x * c + pltpu.roll(x, 8, 1) * sa + pltpu.roll(x, LANE - 8, 1) * sb

    q = _dot(xn, w_ref[:, 0:512])
    yq_ref[:, 0:512] = q
    for j in range(4):
        yq_ref[:, 512 + LANE * j:512 + LANE * (j + 1)] = rope(q[:, LANE * j:LANE * (j + 1)])
    kv = _dot(xn, w_ref[:, 512:1280])
    kc_ref[...] = kv[:, 0:128]
    vc_ref[...] = kv[:, 128:256]
    ks_ref[...] = rope(kv[:, 256:384])
    vs_ref[...] = kv[:, 384:512]
    kw_ref[...] = rope(kv[:, 512:640])
    vw_ref[...] = kv[:, 640:768]
    rest = _dot(xn, w_ref[:, 1280:1920])
    gt_ref[...] = rest[:, 0:128]
    cvcg_ref[...] = rest[:, 128:640]
    rw_ref[...] = _dot(xn, w_ref[:, 1920:IN_PAD])


def _proj(h, nrm, w, rc, rsa, rsb):
    n = h.shape[0]
    tm = _row_tile(n, 512)
    row = lambda wd: pl.BlockSpec((tm, wd), lambda i: (i, 0))
    widths = (1024, 128, 128, 128, 128, 128, 128, 128, 512, RW_COLS)
    return pl.pallas_call(
        _proj_body, grid=(n // tm,),
        in_specs=[row(D_MODEL), _full((1, D_MODEL)), _full((D_MODEL, IN_PAD)), row(LANE), row(LANE), row(LANE)],
        out_specs=[row(wd) for wd in widths],
        out_shape=[jax.ShapeDtypeStruct((n, wd), F32) for wd in widths],
        compiler_params=_params(("parallel",), VMEM_BIG), name="proj")(h, nrm, w, rc, rsa, rsb)


def _cmp_body(x_ref, pe_ref, w_ref, o_ref):
    x = x_ref[...]
    o_ref[:, 0:128] = _dot((x + pe_ref[0:1, :]).astype(BF16), w_ref[:, 0:128])
    o_ref[:, 128:256] = _dot((x + pe_ref[1:2, :]).astype(BF16), w_ref[:, 128:256])


def _compress(x3, layer, pe2, wbig):
    r = x3.shape[1]
    tm = _row_tile(r, 512)
    return pl.pallas_call(
        _cmp_body, grid=(r // tm,),
        in_specs=[pl.BlockSpec((None, tm, 2048), lambda i: (layer, i, 0)), _full((2, 2048)), _full((2048, 256))],
        out_specs=pl.BlockSpec((tm, 256), lambda i: (i, 0)),
        out_shape=jax.ShapeDtypeStruct((r, 256), F32),
        compiler_params=_params(("parallel",), VMEM_BIG), name="compress")(x3, pe2, wbig)


def _masked_softmax(s, mask):
    s = jnp.where(mask, s, NEG)
    m = jnp.max(s, axis=-1, keepdims=True)
    e = jnp.where(mask, jnp.exp(s - m), 0.0)
    d = jnp.sum(e, axis=-1, keepdims=True)
    return e / jnp.maximum(d, 1e-30)


def _topk_mask(score, jb, n_rows, n_top):
    rank = jnp.zeros(score.shape, F32)
    for i in range(n_rows):
        row = score[i:i + 1, :]
        beats = (row > score) | ((row >= score) & (jb > i))
        rank = rank + jnp.where(beats, 1.0, 0.0)
    return jnp.where((rank < n_top) & (jb < n_rows), 1.0, 0.0)


def _nsa_prompt_body(yq_ref, gt_ref, abk_ref, abv_ref, ks_ref, vs_ref, kw_ref, vw_ref, ovt_ref, e_ref, ge_ref,
                     o_ref, *, seq):
    nc = seq // CMP_STRIDE
    ns = seq // SEL_BLOCK
    n_top = min(SEL_TOP, ns)
    q0 = pl.program_id(1) * QB
    low = lax.broadcasted_iota(jnp.int32, (QB, LANE), 1) < HEAD_DIM
    qpos = q0 + lax.broadcasted_iota(jnp.int32, (QB, 1), 0)
    qpos4 = jnp.concatenate([qpos] * HPG, axis=0)
    scale = HEAD_DIM ** -0.5

    gexp = _dot2(jax.nn.sigmoid(gt_ref[...]), ge_ref[...])
    yq = yq_ref[...]

    def qstack(base, g):
        rows = []
        for hh in range(HPG):
            h = HPG * g + hh
            chunk = yq[:, base + (h // 2) * LANE:base + (h // 2 + 1) * LANE]
            if (h % 2 == 1) != (g == 1):
                chunk = pltpu.roll(chunk, HEAD_DIM, 1)
            rows.append(jnp.where(low if g == 0 else ~low, chunk, 0.0))
        return (jnp.concatenate(rows, axis=0) * scale).astype(BF16)

    abk, abv = abk_ref[...], abv_ref[...]
    kcf = (abk[:, :LANE] + pltpu.roll(abk[:, LANE:], nc - 1, 0)).astype(BF16)
    vcf = (abv[:, :LANE] + pltpu.roll(abv[:, LANE:], nc - 1, 0)).astype(BF16)
    ncol = lax.broadcasted_iota(jnp.int32, (1, nc), 1)
    cmp_valid = ((ncol * CMP_STRIDE + CMP_BLOCK - 1) <= qpos4) & (ncol < nc - 1)

    jb = lax.broadcasted_iota(jnp.int32, (ns, QB), 0)
    cur = (q0 + lax.broadcasted_iota(jnp.int32, (ns, QB), 1)) // SEL_BLOCK
    blk_valid = jb <= cur
    blk_forced = (jb == 0) | (jb == cur) | (jb == cur - 1)

    w0 = pl.multiple_of(jnp.maximum(q0 - WINDOW, 0), QB)
    wl = WINDOW + QB
    kwb = kw_ref[pl.ds(w0, wl), :].astype(BF16)
    vwb = vw_ref[pl.ds(w0, wl), :].astype(BF16)
    kp = w0 + lax.broadcasted_iota(jnp.int32, (1, wl), 1)
    win_valid = (kp >= qpos4 - WINDOW) & (kp <= qpos4)
    n_kt = (q0 + QB + KT - 1) // KT

    for g in range(A_KV_HEADS):
        qc = qstack(0, g)
        qr = qstack(A_WIDTH, g)
        p = _masked_softmax(_dot_nt(qc, kcf), cmp_valid)
        o_cmp = _dot(p.astype(BF16), vcf)
        pg = p[0:QB] + p[QB:2 * QB] + p[2 * QB:3 * QB] + p[3 * QB:4 * QB]
        pg_hi, pg_lo = _split(pg)
        imp_t = _dot_nt(ovt_ref[...], pg_hi) + _dot_nt(ovt_ref[...], pg_lo)
        score = jnp.where(blk_valid, jnp.where(blk_forced, FORCE_SCORE, imp_t), -1.0)
        sel_t = _topk_mask(score, jb, ns, n_top)
        if ns < LANE:
            sel_t = jnp.concatenate([sel_t, jnp.zeros((LANE - ns, QB), F32)], axis=0)
        sel = sel_t.T.astype(BF16)

        def kv_step(kt, carry, qr=qr, sel=sel):
            m, l, acc = carry
            k0 = pl.multiple_of(kt * KT, KT)
            kb = ks_ref[pl.ds(k0, KT), :].astype(BF16)
            vb = vs_ref[pl.ds(k0, KT), :].astype(BF16)
            s = _dot_nt(qr, kb)
            kpos = k0 + lax.broadcasted_iota(jnp.int32, (1, KT), 1)
            mk = (_dot(sel, e_ref[kt]) > 0.5) & (kpos <= qpos)
            s3 = jnp.where(mk[None], s.reshape(HPG, QB, KT), NEG)
            m_new = jnp.maximum(m, jnp.max(s3, axis=-1, keepdims=True))
            alpha = jnp.exp(m - m_new)
            pe = jnp.exp(s3 - m_new)
            l = alpha * l + jnp.sum(pe, axis=-1, keepdims=True)
            pv = _dot(pe.reshape(HPG * QB, KT).astype(BF16), vb)
            return m_new, l, alpha * acc + pv.reshape(HPG, QB, LANE)

        init = (jnp.full((HPG, QB, 1), NEG, F32), jnp.zeros((HPG, QB, 1), F32), jnp.zeros((HPG, QB, LANE), F32))
        _, l, acc = lax.fori_loop(0, n_kt, kv_step, init)
        o_sel = (acc / l).reshape(HPG * QB, LANE)
        pw = _masked_softmax(_dot_nt(qr, kwb), win_valid)
        o_win = _dot(pw.astype(BF16), vwb)

        for jj in range(2):
            def place(o, jj=jj, g=g):
                a = o[(2 * jj) * QB:(2 * jj + 1) * QB]
                b = o[(2 * jj + 1) * QB:(2 * jj + 2) * QB]
                if g == 0:
                    b = pltpu.roll(b, HEAD_DIM, 1)
                else:
                    a = pltpu.roll(a, HEAD_DIM, 1)
                return jnp.where(low, a, b)
            j = 2 * g + jj
            o_ref[:, j * LANE:(j + 1) * LANE] = (
                gexp[:, j * LANE:(j + 1) * LANE] * place(o_cmp)
                + gexp[:, A_WIDTH + j * LANE:A_WIDTH + (j + 1) * LANE] * place(o_sel)
                + gexp[:, 2 * A_WIDTH + j * LANE:2 * A_WIDTH + (j + 1) * LANE] * place(o_win))


def _nsa_prompt(yq, gt, abk, abv, ks, vs, kw, vw, ovt, emat, ge, batch, seq):
    nq = seq // QB
    nc = seq // CMP_STRIDE
    qrow = lambda wd: pl.BlockSpec((QB, wd), lambda b, i: (b * nq + i, 0))
    per_b = lambda r, wd: pl.BlockSpec((r, wd), lambda b, i: (b, 0))
    return pl.pallas_call(
        functools.partial(_nsa_prompt_body, seq=seq), grid=(batch, nq),
        in_specs=[qrow(2 * A_WIDTH), qrow(LANE), per_b(nc, 256), per_b(nc, 256), per_b(seq, LANE), per_b(seq, LANE),
                  per_b(seq, LANE), per_b(seq, LANE), _full(ovt.shape), _full(emat.shape), _full(ge.shape)],
        out_specs=qrow(A_WIDTH), out_shape=jax.ShapeDtypeStruct((batch * seq, A_WIDTH), F32),
        compiler_params=_params(("parallel", "arbitrary"), VMEM_BIG), name="nsa_prompt")(
            yq, gt, abk, abv, ks, vs, kw, vw, ovt, emat, ge)


def _nsa_decode_body(pt_ref, q2_ref, gt_ref, ksn_ref, vsn_ref, kwn_ref, vwn_ref, wink_ref, winv_ref, ovt_ref,
                     e_ref, *rest, n_pages, page):
    del pt_ref
    abk_refs = rest[0:n_pages]
    abv_refs = rest[n_pages:2 * n_pages]
    sk_refs = rest[2 * n_pages:3 * n_pages]
    sv_refs = rest[3 * n_pages:4 * n_pages]
    o_ref = rest[4 * n_pages]
    past = n_pages * page
    ncd = past // CMP_STRIDE
    nsd = past // SEL_BLOCK + 1
    nsp = -(-nsd // 8) * 8
    n_top = min(SEL_TOP, nsd)
    cur = past // SEL_BLOCK
    scale = HEAD_DIM ** -0.5
    q2 = q2_ref[...] * scale
    qc = q2[0:A_HEADS].astype(BF16)
    qr_f = q2[A_HEADS:2 * A_HEADS]
    qr = qr_f.astype(BF16)
    rowi = lax.broadcasted_iota(jnp.int32, (A_HEADS, 1), 0)

    abk = jnp.concatenate([r[...] for r in abk_refs], axis=0)
    abv = jnp.concatenate([r[...] for r in abv_refs], axis=0)
    kcf = (abk[:, :LANE] + pltpu.roll(abk[:, LANE:], ncd - 1, 0)).astype(BF16)
    vcf = (abv[:, :LANE] + pltpu.roll(abv[:, LANE:], ncd - 1, 0)).astype(BF16)
    ncol = lax.broadcasted_iota(jnp.int32, (A_HEADS, ncd), 1)
    p = _masked_softmax(_dot_nt(qc, kcf), ncol < ncd - 1)
    o_cmp = _dot(p.astype(BF16), vcf)
    pg0 = jnp.sum(p[0:HPG], axis=0, keepdims=True)
    pg1 = jnp.sum(p[HPG:2 * HPG], axis=0, keepdims=True)
    rows128 = lax.broadcasted_iota(jnp.int32, (LANE, 1), 0)
    pg = jnp.where(rows128 == 0, pg0, jnp.where(rows128 == 1, pg1, 0.0))
    pg_hi, pg_lo = _split(pg)
    imp_t = _dot_nt(ovt_ref[...], pg_hi) + _dot_nt(ovt_ref[...], pg_lo)
    jb = lax.broadcasted_iota(jnp.int32, (nsp, LANE), 0)
    blk_forced = (jb == 0) | (jb == cur) | (jb == cur - 1)
    score = jnp.where(jb <= cur, jnp.where(blk_forced, FORCE_SCORE, imp_t), -1.0)
    sel_t = _topk_mask(score, jb, nsd, n_top)
    sel_t = jnp.concatenate([sel_t, jnp.zeros((LANE - nsp, LANE), F32)], axis=0)
    sel = sel_t.T
    selh = jnp.where(rowi < HPG, sel[0:1], sel[1:2]).astype(BF16)
    mask = _dot(selh, e_ref[...]) > 0.5

    s = jnp.concatenate([_dot_nt(qr, r[...].astype(BF16)) for r in sk_refs], axis=1)
    s = jnp.where(mask, s, NEG)
    s_new = jnp.sum(qr_f * ksn_ref[...], axis=1, keepdims=True)
    m = jnp.maximum(jnp.max(s, axis=1, keepdims=True), s_new)
    e = jnp.exp(s - m)
    e_new = jnp.exp(s_new - m)
    d = jnp.sum(e, axis=1, keepdims=True) + e_new
    acc = e_new * vsn_ref[...]
    for j, r in enumerate(sv_refs):
        acc = acc + _dot(e[:, j * page:(j + 1) * page].astype(BF16), r[...].astype(BF16))
    o_sel = acc / d

    s = _dot_nt(qr, wink_ref[...].astype(BF16))
    s_new = jnp.sum(qr_f * kwn_ref[...], axis=1, keepdims=True)
    m = jnp.maximum(jnp.max(s, axis=1, keepdims=True), s_new)
    e = jnp.exp(s - m)
    e_new = jnp.exp(s_new - m)
    d = jnp.sum(e, axis=1, keepdims=True) + e_new
    o_win = (_dot(e.astype(BF16), winv_ref[...].astype(BF16)) + e_new * vwn_ref[...]) / d

    sg = jax.nn.sigmoid(gt_ref[...])
    lane8 = lax.broadcasted_iota(jnp.int32, (A_HEADS, LANE), 1)
    gcol = lambda c: jnp.sum(jnp.where(lane8 == rowi * 3 + c, sg, 0.0), axis=1, keepdims=True)
    o8 = gcol(0) * o_cmp + gcol(1) * o_sel + gcol(2) * o_win
    low = lax.broadcasted_iota(jnp.int32, (1, LANE), 1) < HEAD_DIM
    for j in range(A_HEADS // 2):
        a, b = o8[2 * j:2 * j + 1], o8[2 * j + 1:2 * j + 2]
        if (2 * j) // HPG == 0:
            b = pltpu.roll(b, HEAD_DIM, 1)
        else:
            a = pltpu.roll(a, HEAD_DIM, 1)
        o_ref[:, j * LANE:(j + 1) * LANE] = jnp.where(low, a, b)


def _nsa_decode(page_table, q2, gt, ksn, vsn, kwn, vwn, wink, winv, ovt, emat, abk_pool, abv_pool, selk, selv,
                layer):
    nb, n_pages = page_table.shape
    page = selk.shape[2]
    row3 = lambda wd: pl.BlockSpec((None, 1, wd), lambda b, pt: (b, 0, 0))
    win = pl.BlockSpec((None, None, wink.shape[2], LANE), lambda b, pt: (layer, b, 0, 0))
    ab_specs = [pl.BlockSpec((None, page // CMP_STRIDE, 256), lambda b, pt, j=j: (pt[b * n_pages + j], 0, 0))
                for j in range(n_pages)]
    pg_specs = [pl.BlockSpec((None, None, page, LANE), lambda b, pt, j=j: (layer, pt[b * n_pages + j], 0, 0))
                for j in range(n_pages)]
    grid_spec = pltpu.PrefetchScalarGridSpec(
        num_scalar_prefetch=1, grid=(nb,),
        in_specs=[pl.BlockSpec((None, 2 * A_HEADS, LANE), lambda b, pt: (b, 0, 0)), row3(LANE), row3(LANE),
                  row3(LANE), row3(LANE), row3(LANE), win, win,
                  pl.BlockSpec(ovt.shape, lambda b, pt: (0, 0)), pl.BlockSpec(emat.shape, lambda b, pt: (0, 0))]
        + ab_specs + ab_specs + pg_specs + pg_specs,
        out_specs=pl.BlockSpec((None, 1, A_WIDTH), lambda b, pt: (b, 0, 0)))
    return pl.pallas_call(
        functools.partial(_nsa_decode_body, n_pages=n_pages, page=page), grid_spec=grid_spec,
        out_shape=jax.ShapeDtypeStruct((nb, 1, A_WIDTH), F32),
        compiler_params=_params(("arbitrary",), VMEM_BIG), name="nsa_decode")(
            page_table.reshape(-1), q2, gt, ksn, vsn, kwn, vwn, wink, winv, ovt, emat,
            *([abk_pool] * n_pages), *([abv_pool] * n_pages), *([selk] * n_pages), *([selv] * n_pages))


def _conv_tail(c, lng_ref, lnb_ref, pw_ref, pwb_ref):
    mu = jnp.mean(c, axis=-1, keepdims=True)
    cen = c - mu
    var = jnp.mean(cen * cen, axis=-1, keepdims=True)
    y = cen * lax.rsqrt(var + LN_EPS) * lng_ref[...] + lnb_ref[...]
    y = y * jax.nn.sigmoid(y)
    return _dot(y.astype(BF16), pw_ref[...]) + pwb_ref[...]


def _conv_prompt_body(x_ref, cw_ref, cb_ref, lng_ref, lnb_ref, pw_ref, pwb_ref, o_ref, st_ref, u_ref, *, tt):
    halo = 32
    i = pl.program_id(1)

    @pl.when(i == 0)
    def _():
        u_ref[0:halo, :] = jnp.zeros((halo, B_WIDTH), F32)

    x = x_ref[...]
    u_ref[halo:halo + tt, :] = x[:, 0:B_WIDTH] * jax.nn.sigmoid(x[:, B_WIDTH:2 * B_WIDTH])
    acc = jnp.zeros((tt, B_WIDTH), F32) + cb_ref[...]
    for w in range(CONV_W):
        off = halo - (CONV_W - 1) + w
        acc = acc + u_ref[off:off + tt, :] * cw_ref[w:w + 1, :]
    o_ref[...] = _conv_tail(acc, lng_ref, lnb_ref, pw_ref, pwb_ref)
    last = u_ref[tt:tt + halo, :]
    st_ref[...] = last
    u_ref[0:halo, :] = last


def _conv_prompt(cvcg, cw, cb, lng, lnb, pw, pwb, batch, seq):
    tt = _row_tile(seq, 512)
    nt = seq // tt
    vec = _full((1, B_WIDTH))
    return pl.pallas_call(
        functools.partial(_conv_prompt_body, tt=tt), grid=(batch, nt),
        in_specs=[pl.BlockSpec((tt, 2 * B_WIDTH), lambda b, i: (b * nt + i, 0)), _full((32, B_WIDTH)), vec, vec, vec,
                  _full((B_WIDTH, B_WIDTH)), vec],
        out_specs=[pl.BlockSpec((tt, B_WIDTH), lambda b, i: (b * nt + i, 0)),
                   pl.BlockSpec((None, 32, B_WIDTH), lambda b, i: (b, 0, 0))],
        out_shape=[jax.ShapeDtypeStruct((batch * seq, B_WIDTH), F32),
                   jax.ShapeDtypeStruct((batch, 32, B_WIDTH), F32)],
        scratch_shapes=[pltpu.VMEM((tt + 32, B_WIDTH), F32)],
        compiler_params=_params(("parallel", "arbitrary")), name="conv_prompt")(cvcg, cw, cb, lng, lnb, pw, pwb)


def _conv_sample_body(x_ref, st_ref, cw_ref, cb_ref, lng_ref, lnb_ref, pw_ref, pwb_ref, o_ref, new_ref):
    x = x_ref[...]
    u = x[:, 0:B_WIDTH] * jax.nn.sigmoid(x[:, B_WIDTH:2 * B_WIDTH])
    acc = u * cw_ref[CONV_W - 1:CONV_W, :] + cb_ref[...]
    for w in range(CONV_W - 1):
        row = st_ref[:, w, :]
        acc = acc + row * cw_ref[w:w + 1, :]
        if w > 0:
            new_ref[:, w - 1, :] = row
    new_ref[:, CONV_W - 2, :] = u
    o_ref[...] = _conv_tail(acc, lng_ref, lnb_ref, pw_ref, pwb_ref)


def _conv_sample(cvcg, state, layer, cw, cb, lng, lnb, pw, pwb):
    nb = cvcg.shape[0]
    vec = _full((1, B_WIDTH))
    return pl.pallas_call(
        _conv_sample_body, grid=(1,),
        in_specs=[_full((nb, 2 * B_WIDTH)),
                  pl.BlockSpec((None, nb, CONV_W - 1, B_WIDTH), lambda i: (layer, 0, 0, 0)),
                  _full((32, B_WIDTH)), vec, vec, vec, _full((B_WIDTH, B_WIDTH)), vec],
        out_specs=[_full((nb, B_WIDTH)), _full((nb, CONV_W - 1, B_WIDTH))],
        out_shape=[jax.ShapeDtypeStruct((nb, B_WIDTH), F32),
                   jax.ShapeDtypeStruct((nb, CONV_W - 1, B_WIDTH), F32)],
        compiler_params=_params(("arbitrary",)), name="conv_sample")(cvcg, state, cw, cb, lng, lnb, pw, pwb)


def _softplus(x):
    return jnp.maximum(x, 0.0) + jnp.log(1.0 + jnp.exp(-jnp.abs(x)))


def _rwkv_pre(rw, prev, mu_ref, w0_ref, w2_ref, a0_ref, a2_ref, g2_ref, kk_ref, ka_ref, rk_ref, seg_ref, outs):
    xm = rw + (prev - rw) * mu_ref[...]
    r, k, v, lor = xm[:, 0:256], xm[:, 256:512], xm[:, 512:768], xm[:, 768:896]
    seg = lambda x: _dot2(x, seg_ref[...])
    w_log = -_softplus(-(w0_ref[...] + _dot(jnp.tanh(lor).astype(BF16), w2_ref[...]))) - 0.5
    decay = jnp.exp(-jnp.exp(w_log))
    a = jax.nn.sigmoid(a0_ref[...] + _dot(lor.astype(BF16), a2_ref[...]))
    gate = _dot(jax.nn.sigmoid(lor).astype(BF16), g2_ref[...])
    kk = k * kk_ref[...]
    kk = kk / jnp.maximum(jnp.sqrt(seg(kk * kk)), 1e-12)
    kmod = k * (1.0 + (a - 1.0) * ka_ref[...])
    bb = kk * a
    nk_ref, w_ref, b_ref, k_ref, rp_ref, v_ref, vc_ref, vb_ref, gate_ref = outs
    nk_ref[...] = -kk
    w_ref[...] = decay
    b_ref[...] = bb
    k_ref[...] = kmod
    rp_ref[...] = decay * r - kk * seg(bb * r)
    v_ref[...] = v
    vc_ref[...] = v * seg(kmod * r)
    vb_ref[...] = v * seg(r * kmod * rk_ref[...])
    gate_ref[...] = gate


def _rwkv_prep_prompt_body(rw_ref, *refs, tt):
    params, outs, carry_ref = refs[:10], refs[10:19], refs[19]

    @pl.when(pl.program_id(1) == 0)
    def _():
        carry_ref[...] = jnp.zeros(carry_ref.shape, F32)

    rw = rw_ref[...]
    rowi = lax.broadcasted_iota(jnp.int32, (tt, 1), 0)
    prev = jnp.where(rowi == 0, carry_ref[0:1, :], pltpu.roll(rw, 1, 0))
    carry_ref[0:1, :] = rw[tt - 1:tt, :]
    _rwkv_pre(rw, prev, *params, outs)


def _rwkv_prep_sample_body(rw_ref, prev_ref, *refs):
    _rwkv_pre(rw_ref[...], prev_ref[...], *refs[:10], refs[10:19])


def _rwkv_param_specs():
    vec = lambda wd: _full((1, wd))
    return [vec(RW_COLS), vec(C_WIDTH), _full((LANE, C_WIDTH)), vec(C_WIDTH), _full((LANE, C_WIDTH)),
            _full((LANE, C_WIDTH)), vec(C_WIDTH), vec(C_WIDTH), vec(C_WIDTH), _full((C_WIDTH, C_WIDTH))]


def _rwkv_prep_prompt(rw, params, batch, seq):
    tt = _row_tile(seq, 512)
    nt = seq // tt
    row = lambda wd: pl.BlockSpec((tt, wd), lambda b, i: (b * nt + i, 0))
    return pl.pallas_call(
        functools.partial(_rwkv_prep_prompt_body, tt=tt), grid=(batch, nt),
        in_specs=[row(RW_COLS)] + _rwkv_param_specs(),
        out_specs=[row(C_WIDTH)] * 9,
        out_shape=[jax.ShapeDtypeStruct((batch * seq, C_WIDTH), F32)] * 9,
        scratch_shapes=[pltpu.VMEM((8, RW_COLS), F32)],
        compiler_params=_params(("parallel", "arbitrary")), name="rwkv_prep_prompt")(rw, *params)


def _rwkv_prep_sample(rw, prev, layer, params):
    nb = rw.shape[0]
    return pl.pallas_call(
        _rwkv_prep_sample_body, grid=(1,),
        in_specs=[_full((nb, RW_COLS)), pl.BlockSpec((None, nb, RW_COLS), lambda i: (layer, 0, 0))]
        + _rwkv_param_specs(),
        out_specs=[_full((nb, C_WIDTH))] * 9,
        out_shape=[jax.ShapeDtypeStruct((nb, C_WIDTH), F32)] * 9,
        compiler_params=_params(("arbitrary",)), name="rwkv_prep_sample")(rw, prev, *params)


def _wkv_head_step(s, nk, w, bb, kk, rp, vtile, ones):
    sk = _dot((s * nk).astype(BF16), ones)
    o8 = _dot_nt(jnp.broadcast_to(rp, (8, HEAD_DIM)).astype(BF16), s.astype(BF16))
    return s * w + sk * bb + vtile * kk, o8[0:1]


def _wkv_scan_body(nk_ref, w_ref, b_ref, k_ref, rp_ref, v_ref, ones_ref, o_ref, sfin_ref, s_ref, vth_ref, vtl_ref,
                   *, batch, tb):
    step_id = pl.program_id(0)

    @pl.when(step_id == 0)
    def _():
        s_ref[...] = jnp.zeros(s_ref.shape, F32)

    for b in range(batch):
        hi, lo = _split(v_ref[b].T)
        vth_ref[b] = hi
        vtl_ref[b] = lo
    rowi = lax.broadcasted_iota(jnp.int32, (tb, HEAD_DIM), 0)
    ones = ones_ref[...]

    def step(t, carry):
        onehot = jnp.where(rowi == t, 1.0, 0.0).astype(BF16)
        for b in range(batch):
            vcols = _dot(vth_ref[b], onehot) + _dot(vtl_ref[b], onehot)
            nk, w, bb = nk_ref[b, pl.ds(t, 1), :], w_ref[b, pl.ds(t, 1), :], b_ref[b, pl.ds(t, 1), :]
            kk, rp = k_ref[b, pl.ds(t, 1), :], rp_ref[b, pl.ds(t, 1), :]
            for h in range(C_HEADS):
                sl = slice(h * HEAD_DIM, (h + 1) * HEAD_DIM)
                s_new, o_row = _wkv_head_step(s_ref[b, h], nk[:, sl], w[:, sl], bb[:, sl], kk[:, sl], rp[:, sl],
                                              vcols[sl, :], ones)
                s_ref[b, h] = s_new
                o_ref[b, h, pl.ds(t, 1), :] = o_row
        return carry

    lax.fori_loop(0, tb, step, 0)

    @pl.when(step_id == pl.num_programs(0) - 1)
    def _():
        sfin_ref[...] = s_ref[...]


def _wkv_scan(nk, w, bb, kk, rp, v, ones, batch, seq):
    tb = _row_tile(seq, 128)
    blk = pl.BlockSpec((batch, tb, C_WIDTH), lambda i: (0, i, 0))
    r3 = lambda x: x.reshape(batch, seq, C_WIDTH)
    return pl.pallas_call(
        functools.partial(_wkv_scan_body, batch=batch, tb=tb), grid=(seq // tb,),
        in_specs=[blk] * 6 + [_full((HEAD_DIM, HEAD_DIM))],
        out_specs=[pl.BlockSpec((batch, C_HEADS, tb, HEAD_DIM), lambda i: (0, 0, i, 0)),
                   _full((batch, C_HEADS, HEAD_DIM, HEAD_DIM))],
        out_shape=[jax.ShapeDtypeStruct((batch, C_HEADS, seq, HEAD_DIM), F32),
                   jax.ShapeDtypeStruct((batch, C_HEADS, HEAD_DIM, HEAD_DIM), F32)],
        scratch_shapes=[pltpu.VMEM((batch, C_HEADS, HEAD_DIM, HEAD_DIM), F32),
                        pltpu.VMEM((batch, C_WIDTH, tb), BF16), pltpu.VMEM((batch, C_WIDTH, tb), BF16)],
        compiler_params=_params(("arbitrary",)), name="wkv_scan")(r3(nk), r3(w), r3(bb), r3(kk), r3(rp), r3(v), ones)


def _wkv_sample_body(nk_ref, w_ref, b_ref, k_ref, rp_ref, v_ref, ones_ref, st_ref, o_ref, new_ref, vth_ref, vtl_ref,
                     *, nb, bs):
    base = pl.program_id(0) * bs
    hi, lo = _split(v_ref[...].T)
    vth_ref[...] = hi
    vtl_ref[...] = lo
    rowi = lax.broadcasted_iota(jnp.int32, (nb, HEAD_DIM), 0)
    ones = ones_ref[...]

    def step(j, carry):
        t = base + j
        onehot = jnp.where(rowi == t, 1.0, 0.0).astype(BF16)
        vcols = _dot(vth_ref[...], onehot) + _dot(vtl_ref[...], onehot)
        nk, w, bb = nk_ref[pl.ds(t, 1), :], w_ref[pl.ds(t, 1), :], b_ref[pl.ds(t, 1), :]
        kk, rp = k_ref[pl.ds(t, 1), :], rp_ref[pl.ds(t, 1), :]
        for h in range(C_HEADS):
            sl = slice(h * HEAD_DIM, (h + 1) * HEAD_DIM)
            s_new, o_row = _wkv_head_step(st_ref[j, h], nk[:, sl], w[:, sl], bb[:, sl], kk[:, sl], rp[:, sl],
                                          vcols[sl, :], ones)
            new_ref[j, h] = s_new
            o_ref[h, pl.ds(t, 1), :] = o_row
        return carry

    lax.fori_loop(0, bs, step, 0)


def _wkv_sample(nk, w, bb, kk, rp, v, ones, state, layer):
    nb = nk.shape[0]
    bs = _row_tile(nb, 32)
    vec = _full((nb, C_WIDTH))
    return pl.pallas_call(
        functools.partial(_wkv_sample_body, nb=nb, bs=bs), grid=(nb // bs,),
        in_specs=[vec] * 6 + [_full((HEAD_DIM, HEAD_DIM)),
                              pl.BlockSpec((None, bs, C_HEADS, HEAD_DIM, HEAD_DIM), lambda i: (layer, i, 0, 0, 0))],
        out_specs=[_full((C_HEADS, nb, HEAD_DIM)),
                   pl.BlockSpec((bs, C_HEADS, HEAD_DIM, HEAD_DIM), lambda i: (i, 0, 0, 0))],
        out_shape=[jax.ShapeDtypeStruct((C_HEADS, nb, HEAD_DIM), F32),
                   jax.ShapeDtypeStruct((nb, C_HEADS, HEAD_DIM, HEAD_DIM), F32)],
        scratch_shapes=[pltpu.VMEM((C_WIDTH, nb), BF16), pltpu.VMEM((C_WIDTH, nb), BF16)],
        compiler_params=_params(("arbitrary",)), name="wkv_sample")(nk, w, bb, kk, rp, v, ones, state)


def _merge_body(h_ref, oa_ref, ob_ref, os_ref, vc_ref, vb_ref, gate_ref, lnw_ref, lnb_ref, gn_ref, n3_ref, wout_ref,
                seg_ref, o_ref):
    seg = lambda x: _dot2(x, seg_ref[...])
    o = os_ref[...] + vc_ref[...]
    cen = o - seg(o) * (1.0 / HEAD_DIM)
    var = seg(cen * cen) * (1.0 / HEAD_DIM)
    o = cen * lax.rsqrt(var + GN_EPS) * lnw_ref[...] + lnb_ref[...]
    o = (o + vb_ref[...]) * gate_ref[...]
    gn = gn_ref[...]
    merged = jnp.concatenate([_rms(oa_ref[...], gn[:, 0:512]), _rms(ob_ref[...], gn[:, 512:768]),
                              _rms(o, gn[:, 768:1024])], axis=1).astype(BF16)
    o_ref[...] = h_ref[...] + _rms(_dot(merged, wout_ref[...]), n3_ref[...])


def _merge(h, oa, ob, os_, vc, vb, gate, lnw, lnb, gn, n3, wout, seg):
    n = h.shape[0]
    tm = _row_tile(n, 512)
    row = lambda wd: pl.BlockSpec((tm, wd), lambda i: (i, 0))
    vec = lambda wd: _full((1, wd))
    return pl.pallas_call(
        _merge_body, grid=(n // tm,),
        in_specs=[row(D_MODEL), row(A_WIDTH), row(B_WIDTH), row(C_WIDTH), row(C_WIDTH), row(C_WIDTH), row(C_WIDTH),
                  vec(C_WIDTH), vec(C_WIDTH), vec(D_MODEL), vec(D_MODEL), _full((D_MODEL, D_MODEL)),
                  _full((C_WIDTH, C_WIDTH))],
        out_specs=row(D_MODEL), out_shape=jax.ShapeDtypeStruct((n, D_MODEL), F32),
        compiler_params=_params(("parallel",), VMEM_BIG), name="merge")(
            h, oa, ob, os_, vc, vb, gate, lnw, lnb, gn, n3, wout, seg)


def _rope_tables(pos):
    half = ROT_DIM // 2
    inv = ROPE_THETA ** (-jnp.arange(half, dtype=F32) * 2.0 / ROT_DIM)
    ang = pos.astype(F32)[:, None] * inv[None, :]
    cos, sin = jnp.cos(ang), jnp.sin(ang)
    n = pos.shape[0]
    one = jnp.ones((n, HEAD_DIM - ROT_DIM), F32)
    zero = jnp.zeros((n, HEAD_DIM - ROT_DIM), F32)
    zh = jnp.zeros((n, half), F32)
    c = jnp.concatenate([cos, cos, one], axis=1)
    sa = jnp.concatenate([zh, sin, zero], axis=1)
    sb = jnp.concatenate([-sin, zh, zero], axis=1)
    tile = lambda x: jnp.concatenate([x, x], axis=1)
    return tile(c), tile(sa), tile(sb)


def _overlap_t(n_cmp, n_sel, rows, cols):
    ci = np.arange(n_cmp)[None, :] * CMP_STRIDE
    sj = np.arange(n_sel)[:, None] * SEL_BLOCK
    ov = ((ci < sj + SEL_BLOCK) & (ci + CMP_BLOCK > sj)).astype(np.float32)
    out = np.zeros((rows, cols), np.float32)
    out[:n_sel, :n_cmp] = ov
    return jnp.asarray(out, BF16)


def _expand_tiles(seq):
    nt = -(-seq // KT)
    key = np.arange(nt * KT).reshape(nt, 1, KT)
    blk = np.arange(LANE).reshape(1, LANE, 1)
    return jnp.asarray((key // SEL_BLOCK == blk).astype(np.float32), BF16)


def _gate_expand():
    ge = np.zeros((LANE, 3 * A_WIDTH), np.float32)
    for h in range(A_HEADS):
        for c in range(3):
            ge[h * 3 + c, c * A_WIDTH + h * HEAD_DIM:c * A_WIDTH + (h + 1) * HEAD_DIM] = 1.0
    return jnp.asarray(ge, BF16)


def _seg_ones():
    idx = np.arange(C_WIDTH) // HEAD_DIM
    return jnp.asarray((idx[:, None] == idx[None, :]).astype(np.float32), BF16)


def _cmp_weights(cmp_w_l, cmp_pe_l):
    eye = jnp.eye(A_KV_HEADS, dtype=F32)
    outs = []
    for i in range(2):
        w4 = cmp_w_l[i].reshape(2, CMP_STRIDE, HEAD_DIM, HEAD_DIM)
        wbig = jnp.einsum('hlde,gk->lgdhke', w4, eye).reshape(CMP_STRIDE * KV_WIDTH, 2 * KV_WIDTH)
        pe = cmp_pe_l[i].reshape(2, CMP_STRIDE, 1, HEAD_DIM)
        pe2 = jnp.broadcast_to(pe, (2, CMP_STRIDE, A_KV_HEADS, HEAD_DIM)).reshape(2, CMP_STRIDE * KV_WIDTH)
        outs.append((pe2, wbig.astype(BF16)))
    return outs


def _lora_pad(w, start):
    return jnp.zeros((LANE, C_WIDTH), F32).at[start:start + w.shape[0]].set(w).astype(BF16)


def kernel(x_prompt, x_sample, cache_cmp_k, cache_cmp_v, cache_sel_k, cache_sel_v, cache_win_k, cache_win_v, state_conv, state_shift, state_wkv, page_table, w_in, w_out, norms, grp_norm, ffn_gate, ffn_up, ffn_down, cmp_pe, cmp_w, conv_w, conv_b, conv_ln_g, conv_ln_b, conv_pw, conv_pw_b, rw_mu, rw_w0, rw_w2, rw_a0, rw_a2, rw_g2, rw_kk, rw_ka, rw_rk, rw_ln_w, rw_ln_b):
    batch, seq, _ = x_prompt.shape
    nb = x_sample.shape[0]
    depth = w_in.shape[0]
    n_pool, page = cache_cmp_k.shape[1], cache_cmp_k.shape[2]
    n_pages = page_table.shape[1]
    past = n_pages * page
    win_buf = cache_win_k.shape[2]
    n_p = batch * seq

    rope_p = _rope_tables(jnp.tile(jnp.arange(seq), batch))
    rope_s = _rope_tables(jnp.full((nb,), past))
    nc, ns = seq // CMP_STRIDE, seq // SEL_BLOCK
    ovt_p = _overlap_t(nc - 1, ns, ns, nc)
    emat_p = _expand_tiles(seq)
    ncd, nsd = past // CMP_STRIDE, past // SEL_BLOCK + 1
    ovt_s = _overlap_t(ncd - 1, nsd, -(-nsd // 8) * 8, ncd)
    emat_s = jnp.asarray((np.arange(past)[None, :] // SEL_BLOCK == np.arange(LANE)[:, None]).astype(np.float32), BF16)
    ge = _gate_expand()
    seg = _seg_ones()
    ones64 = jnp.ones((HEAD_DIM, HEAD_DIM), BF16)

    pool_k = cache_cmp_k.reshape(depth, n_pool * page // CMP_STRIDE, CMP_STRIDE * KV_WIDTH)
    pool_v = cache_cmp_v.reshape(depth, n_pool * page // CMP_STRIDE, CMP_STRIDE * KV_WIDTH)
    selk = cache_sel_k.reshape(depth, n_pool, page, KV_WIDTH)
    selv = cache_sel_v.reshape(depth, n_pool, page, KV_WIDTH)
    wink = cache_win_k.reshape(depth, nb, win_buf, KV_WIDTH)
    winv = cache_win_v.reshape(depth, nb, win_buf, KV_WIDTH)

    hp = x_prompt.reshape(n_p, D_MODEL)
    hs = x_sample.reshape(nb, D_MODEL)
    sp = {k: [] for k in ('cmp_k', 'cmp_v', 'sel_k', 'sel_v', 'win_k', 'win_v', 'conv', 'shift', 'wkv')}
    ss = {k: [] for k in sp}
    vec = lambda x: x.reshape(1, -1)
    grp = jnp.arange(A_HEADS) // HPG

    for l in range(depth):
        nr = norms[l]
        wg, wu, wd = ffn_gate[l].astype(BF16), ffn_up[l].astype(BF16), ffn_down[l].astype(BF16)
        w_in_l = w_in[l]
        w_in_p = jnp.concatenate([w_in_l[:, :1304], jnp.zeros((D_MODEL, IN_PAD - w_in_l.shape[1]), F32),
                                  w_in_l[:, 1304:]], axis=1).astype(BF16)
        w_out_l = w_out[l].astype(BF16)
        (pe_k, wb_k), (pe_v, wb_v) = _cmp_weights(cmp_w[l], cmp_pe[l])
        cw = jnp.zeros((32, B_WIDTH), F32).at[:CONV_W].set(conv_w[l])
        conv_params = (cw, vec(conv_b[l]), vec(conv_ln_g[l]), vec(conv_ln_b[l]), conv_pw[l].astype(BF16),
                       vec(conv_pw_b[l]))
        rw_params = (vec(rw_mu[l]), vec(rw_w0[l]), _lora_pad(rw_w2[l], 0), vec(rw_a0[l]),
                     _lora_pad(rw_a2[l], LORA_W), _lora_pad(rw_g2[l], LORA_W + LORA_A), vec(rw_kk[l]),
                     vec(rw_ka[l]), vec(rw_rk[l]), seg)
        merge_params = (vec(rw_ln_w[l]), vec(rw_ln_b[l]), vec(grp_norm[l]), vec(nr[3]), w_out_l, seg)

        hp = _ffn(hp, vec(nr[0]), vec(nr[1]), wg[0], wu[0], wd[0])
        yq, kc, vc, ks, vs, kw, vw, gt, cvcg, rw = _proj(hp, vec(nr[2]), w_in_p, *rope_p)
        abk = _compress(kc.reshape(1, n_p // CMP_STRIDE, CMP_STRIDE * KV_WIDTH), 0, pe_k, wb_k)
        abv = _compress(vc.reshape(1, n_p // CMP_STRIDE, CMP_STRIDE * KV_WIDTH), 0, pe_v, wb_v)
        o_a = _nsa_prompt(yq, gt, abk, abv, ks, vs, kw, vw, ovt_p, emat_p, ge, batch, seq)
        o_b, conv_st = _conv_prompt(cvcg, *conv_params, batch, seq)
        nk, wdec, bb, kmod, rp, v, vcc, vbb, gate = _rwkv_prep_prompt(rw, rw_params, batch, seq)
        o_s, s_fin = _wkv_scan(nk, wdec, bb, kmod, rp, v, ones64, batch, seq)
        o_s = o_s.transpose(0, 2, 1, 3).reshape(n_p, C_WIDTH)
        hp = _merge(hp, o_a, o_b, o_s, vcc, vbb, gate, *merge_params)
        hp = _ffn(hp, vec(nr[4]), vec(nr[5]), wg[1], wu[1], wd[1])
        kv5 = lambda x: x.reshape(batch, seq, A_KV_HEADS, HEAD_DIM)
        n_buf = min(WINDOW, seq)
        sp['cmp_k'].append(kv5(kc)); sp['cmp_v'].append(kv5(vc))
        sp['sel_k'].append(kv5(ks)); sp['sel_v'].append(kv5(vs))
        sp['win_k'].append(kv5(kw)[:, -n_buf:]); sp['win_v'].append(kv5(vw)[:, -n_buf:])
        sp['conv'].append(conv_st[:, 32 - (CONV_W - 1):])
        sp['shift'].append(rw.reshape(batch, seq, RW_COLS)[:, -1])
        sp['wkv'].append(s_fin)

        hs = _ffn(hs, vec(nr[0]), vec(nr[1]), wg[0], wu[0], wd[0])
        yq, kc, vc, ks, vs, kw, vw, gt, cvcg, rw = _proj(hs, vec(nr[2]), w_in_p, *rope_s)
        abk_pool = _compress(pool_k, l, pe_k, wb_k).reshape(n_pool, page // CMP_STRIDE, 256)
        abv_pool = _compress(pool_v, l, pe_v, wb_v).reshape(n_pool, page // CMP_STRIDE, 256)
        q8 = yq.reshape(nb, 2, A_HEADS, HEAD_DIM)
        q2 = jnp.concatenate([jnp.where((grp == 0)[None, None, :, None], q8, 0.0),
                              jnp.where((grp == 1)[None, None, :, None], q8, 0.0)], axis=-1)
        q2 = q2.reshape(nb, 2 * A_HEADS, LANE)
        r3 = lambda x: x.reshape(nb, 1, -1)
        o2 = _nsa_decode(page_table, q2, r3(gt), r3(ks), r3(vs), r3(kw), r3(vw), wink, winv, ovt_s, emat_s,
                         abk_pool, abv_pool, selk, selv, l)
        o_a = o2.reshape(nb, A_WIDTH)
        o_b, conv_new = _conv_sample(cvcg, state_conv, l, *conv_params)
        nk, wdec, bb, kmod, rp, v, vcc, vbb, gate = _rwkv_prep_sample(rw, state_shift, l, rw_params)
        o_s, s_new = _wkv_sample(nk, wdec, bb, kmod, rp, v, ones64, state_wkv, l)
        o_s = o_s.transpose(1, 0, 2).reshape(nb, C_WIDTH)
        hs = _merge(hs, o_a, o_b, o_s, vcc, vbb, gate, *merge_params)
        hs = _ffn(hs, vec(nr[4]), vec(nr[5]), wg[1], wu[1], wd[1])
        kv5s = lambda x: x.reshape(nb, 1, A_KV_HEADS, HEAD_DIM)
        ss['cmp_k'].append(kv5s(kc)); ss['cmp_v'].append(kv5s(vc))
        ss['sel_k'].append(kv5s(ks)); ss['sel_v'].append(kv5s(vs))
        ss['win_k'].append(jnp.concatenate([cache_win_k[l], kv5s(kw)], axis=1)[:, -win_buf:])
        ss['win_v'].append(jnp.concatenate([cache_win_v[l], kv5s(vw)], axis=1)[:, -win_buf:])
        ss['conv'].append(conv_new)
        ss['shift'].append(rw)
        ss['wkv'].append(s_new)

    keys = ('cmp_k', 'cmp_v', 'sel_k', 'sel_v', 'win_k', 'win_v', 'conv', 'shift', 'wkv')
    st = lambda d: tuple(jnp.stack(d[k]) for k in keys)
    return (hp.reshape(batch, seq, D_MODEL), hs.reshape(nb, 1, D_MODEL)) + st(sp) + st(ss)
```

```python
import functools

import numpy as np
import jax
import jax.numpy as jnp
from jax import lax
from jax.experimental import pallas as pl
from jax.experimental.pallas import tpu as pltpu

F32 = jnp.float32
BF16 = jnp.bfloat16

D_MODEL = 1024
HEAD_DIM = 64
A_WIDTH = 512
A_HEADS = 8
A_KV_HEADS = 2
KV_WIDTH = 128
HPG = A_HEADS // A_KV_HEADS
CMP_BLOCK = 32
CMP_STRIDE = 16
SEL_BLOCK = 64
SEL_TOP = 16
WINDOW = 512
FORCE_SCORE = 1.0e4
ROT_DIM = 16
ROPE_THETA = 500000.0
B_WIDTH = 256
CONV_W = 31
C_WIDTH = 256
C_HEADS = 4
LORA_W = 32
LORA_A = 32
LORA_G = 64
RW_COLS = 3 * C_WIDTH + LORA_W + LORA_A + LORA_G
D_FF = 2816
RMS_EPS = 1e-6
LN_EPS = 1e-5
GN_EPS = 64e-5

LANE = 128
QB = 128
KT = 512
FF_CHUNK = 1408
IN_PAD = 2816
NEG = -1e30
VMEM_BIG = 56 * 1024 * 1024


def _dot(a, b):
    return jnp.dot(a, b, preferred_element_type=F32)


def _dot_nt(a, b):
    return lax.dot_general(a, b, (((1,), (1,)), ((), ())), preferred_element_type=F32)


def _rms(x, g):
    return x * lax.rsqrt(jnp.mean(x * x, axis=-1, keepdims=True) + RMS_EPS) * g


def _split(x):
    hi = x.astype(BF16)
    lo = (x - hi.astype(F32)).astype(BF16)
    return hi, lo


def _dot2(x, w):
    hi, lo = _split(x)
    return _dot(hi, w) + _dot(lo, w)


def _full(shape):
    nd = len(shape)
    return pl.BlockSpec(shape, lambda *_: (0,) * nd)


def _params(sem, vmem=None):
    return pltpu.CompilerParams(dimension_semantics=sem, vmem_limit_bytes=vmem)


def _row_tile(n, cap):
    t = cap
    while n % t:
        t //= 2
    return t


def _ffn_body(h_ref, npre_ref, npost_ref, wg_ref, wu_ref, wd_ref, o_ref):
    h = h_ref[...]
    xn = _rms(h, npre_ref[...]).astype(BF16)
    acc = jnp.zeros(h.shape, F32)
    for c in range(D_FF // FF_CHUNK):
        sl = slice(c * FF_CHUNK, (c + 1) * FF_CHUNK)
        g = _dot(xn, wg_ref[:, sl])
        u = _dot(xn, wu_ref[:, sl])
        a = (g * jax.nn.sigmoid(g) * u).astype(BF16)
        acc = acc + _dot(a, wd_ref[sl, :])
    o_ref[...] = h + 0.5 * _rms(acc, npost_ref[...])


def _ffn(h, npre, npost, wg, wu, wd):
    n = h.shape[0]
    tm = _row_tile(n, 512)
    row = pl.BlockSpec((tm, D_MODEL), lambda i: (i, 0))
    return pl.pallas_call(
        _ffn_body, grid=(n // tm,),
        in_specs=[row, _full((1, D_MODEL)), _full((1, D_MODEL)), _full((D_MODEL, D_FF)),
                  _full((D_MODEL, D_FF)), _full((D_FF, D_MODEL))],
        out_specs=row, out_shape=jax.ShapeDtypeStruct((n, D_MODEL), F32),
        compiler_params=_params(("parallel",), VMEM_BIG), name="ffn")(h, npre, npost, wg, wu, wd)


def _proj_body(h_ref, n_ref, w_ref, rc_ref, rsa_ref, rsb_ref, yq_ref, kc_ref, vc_ref, ks_ref, vs_ref,
               kw_ref, vw_ref, gt_ref, cvcg_ref, rw_ref):
    xn = _rms(h_ref[...], n_ref[...]).astype(BF16)
    c, sa, sb = rc_ref[...], rsa_ref[...], rsb_ref[...]

    def rope(x):
        return x * c + pltpu.roll(x, 8, 1) * sa + pltpu.roll(x, LANE - 8, 1) * sb

    q = _dot(xn, w_ref[:, 0:512])
    yq_ref[:, 0:512] = q
    for j in range(4):
        yq_ref[:, 512 + LANE * j:512 + LANE * (j + 1)] = rope(q[:, LANE * j:LANE * (j + 1)])
    kv = _dot(xn, w_ref[:, 512:1280])
    kc_ref[...] = kv[:, 0:128]
    vc_ref[...] = kv[:, 128:256]
    ks_ref[...] = rope(kv[:, 256:384])
    vs_ref[...] = kv[:, 384:512]
    kw_ref[...] = rope(kv[:, 512:640])
    vw_ref[...] = kv[:, 640:768]
    rest = _dot(xn, w_ref[:, 1280:1920])
    gt_ref[...] = rest[:, 0:128]
    cvcg_ref[...] = rest[:, 128:640]
    rw_ref[...] = _dot(xn, w_ref[:, 1920:IN_PAD])


def _proj(h, nrm, w, rc, rsa, rsb):
    n = h.shape[0]
    tm = _row_tile(n, 512)
    row = lambda wd: pl.BlockSpec((tm, wd), lambda i: (i, 0))
    widths = (1024, 128, 128, 128, 128, 128, 128, 128, 512, RW_COLS)
    return pl.pallas_call(
        _proj_body, grid=(n // tm,),
        in_specs=[row(D_MODEL), _full((1, D_MODEL)), _full((D_MODEL, IN_PAD)), row(LANE), row(LANE), row(LANE)],
        out_specs=[row(wd) for wd in widths],
        out_shape=[jax.ShapeDtypeStruct((n, wd), F32) for wd in widths],
        compiler_params=_params(("parallel",), VMEM_BIG), name="proj")(h, nrm, w, rc, rsa, rsb)


def _cmp_body(x_ref, pe_ref, w_ref, o_ref):
    x = x_ref[...]
    o_ref[:, 0:128] = _dot((x + pe_ref[0:1, :]).astype(BF16), w_ref[:, 0:128])
    o_ref[:, 128:256] = _dot((x + pe_ref[1:2, :]).astype(BF16), w_ref[:, 128:256])


def _compress(x3, layer, pe2, wbig):
    r = x3.shape[1]
    tm = _row_tile(r, 512)
    return pl.pallas_call(
        _cmp_body, grid=(r // tm,),
        in_specs=[pl.BlockSpec((None, tm, 2048), lambda i: (layer, i, 0)), _full((2, 2048)), _full((2048, 256))],
        out_specs=pl.BlockSpec((tm, 256), lambda i: (i, 0)),
        out_shape=jax.ShapeDtypeStruct((r, 256), F32),
        compiler_params=_params(("parallel",), VMEM_BIG), name="compress")(x3, pe2, wbig)


def _cmp_pool_body(x_ref, pe_ref, w_ref, o_ref, xs_ref, *, pb):
    def to_token_major(j, carry):
        xs_ref[pl.ds(pl.multiple_of(j * LANE, LANE), LANE), :] = x_ref[j].T
        return carry

    lax.fori_loop(0, pb, to_token_major, 0)
    n_chunk = o_ref.shape[0]
    for h in range(2):
        acc = jnp.zeros((n_chunk, LANE), F32)
        for t in range(CMP_STRIDE):
            i = h * CMP_STRIDE + t
            xt = xs_ref[pl.ds(t, n_chunk, stride=CMP_STRIDE), :]
            acc = acc + _dot((xt + pe_ref[i:i + 1, :]).astype(BF16), w_ref[i])
        o_ref[:, h * LANE:(h + 1) * LANE] = acc


def _compress_pool(pool_t, layer, pe2, wbig):
    n_pool, page = pool_t.shape[1], pool_t.shape[3]
    pb = _row_tile(n_pool, 64)
    w32 = wbig.reshape(CMP_STRIDE, LANE, 2, LANE).transpose(2, 0, 1, 3).reshape(2 * CMP_STRIDE, LANE, LANE)
    pe32 = pe2.reshape(2 * CMP_STRIDE, LANE)
    n_chunk = pb * page // CMP_STRIDE
    return pl.pallas_call(
        functools.partial(_cmp_pool_body, pb=pb), grid=(n_pool // pb,),
        in_specs=[pl.BlockSpec((None, pb, LANE, page), lambda i: (layer, i, 0, 0)), _full(pe32.shape),
                  _full(w32.shape)],
        out_specs=pl.BlockSpec((n_chunk, 256), lambda i: (i, 0)),
        out_shape=jax.ShapeDtypeStruct((n_pool * page // CMP_STRIDE, 256), F32),
        scratch_shapes=[pltpu.VMEM((pb * page, LANE), F32)],
        compiler_params=_params(("parallel",), VMEM_BIG), name="compress_pool")(pool_t, pe32, w32)


def _masked_softmax(s, mask):
    s = jnp.where(mask, s, NEG)
    m = jnp.max(s, axis=-1, keepdims=True)
    e = jnp.where(mask, jnp.exp(s - m), 0.0)
    d = jnp.sum(e, axis=-1, keepdims=True)
    return e / jnp.maximum(d, 1e-30)


def _topk_mask(score, jb, n_rows, n_top):
    rank = jnp.zeros(score.shape, F32)
    for i in range(n_rows):
        row = score[i:i + 1, :]
        beats = (row > score) | ((row >= score) & (jb > i))
        rank = rank + jnp.where(beats, 1.0, 0.0)
    return jnp.where((rank < n_top) & (jb < n_rows), 1.0, 0.0)


def _nsa_prompt_body(yq_ref, gt_ref, abk_ref, abv_ref, ks_ref, vs_ref, kw_ref, vw_ref, ovt_ref, e_ref, ge_ref,
                     o_ref, *, seq):
    nc = seq // CMP_STRIDE
    ns = seq // SEL_BLOCK
    n_top = min(SEL_TOP, ns)
    q0 = pl.program_id(1) * QB
    low = lax.broadcasted_iota(jnp.int32, (QB, LANE), 1) < HEAD_DIM
    qpos = q0 + lax.broadcasted_iota(jnp.int32, (QB, 1), 0)
    qpos4 = jnp.concatenate([qpos] * HPG, axis=0)
    scale = HEAD_DIM ** -0.5

    gexp = _dot2(jax.nn.sigmoid(gt_ref[...]), ge_ref[...])
    yq = yq_ref[...]

    def qstack(base, g):
        rows = []
        for hh in range(HPG):
            h = HPG * g + hh
            chunk = yq[:, base + (h // 2) * LANE:base + (h // 2 + 1) * LANE]
            if (h % 2 == 1) != (g == 1):
                chunk = pltpu.roll(chunk, HEAD_DIM, 1)
            rows.append(jnp.where(low if g == 0 else ~low, chunk, 0.0))
        return (jnp.concatenate(rows, axis=0) * scale).astype(BF16)

    abk, abv = abk_ref[...], abv_ref[...]
    kcf = (abk[:, :LANE] + pltpu.roll(abk[:, LANE:], nc - 1, 0)).astype(BF16)
    vcf = (abv[:, :LANE] + pltpu.roll(abv[:, LANE:], nc - 1, 0)).astype(BF16)
    ncol = lax.broadcasted_iota(jnp.int32, (1, nc), 1)
    cmp_valid = ((ncol * CMP_STRIDE + CMP_BLOCK - 1) <= qpos4) & (ncol < nc - 1)

    jb = lax.broadcasted_iota(jnp.int32, (ns, QB), 0)
    cur = (q0 + lax.broadcasted_iota(jnp.int32, (ns, QB), 1)) // SEL_BLOCK
    blk_valid = jb <= cur
    blk_forced = (jb == 0) | (jb == cur) | (jb == cur - 1)

    w0 = pl.multiple_of(jnp.maximum(q0 - WINDOW, 0), QB)
    wl = WINDOW + QB
    kwb = kw_ref[pl.ds(w0, wl), :].astype(BF16)
    vwb = vw_ref[pl.ds(w0, wl), :].astype(BF16)
    kp = w0 + lax.broadcasted_iota(jnp.int32, (1, wl), 1)
    win_valid = (kp >= qpos4 - WINDOW) & (kp <= qpos4)
    n_kt = (q0 + QB + KT - 1) // KT

    for g in range(A_KV_HEADS):
        qc = qstack(0, g)
        qr = qstack(A_WIDTH, g)
        p = _masked_softmax(_dot_nt(qc, kcf), cmp_valid)
        o_cmp = _dot(p.astype(BF16), vcf)
        pg = p[0:QB] + p[QB:2 * QB] + p[2 * QB:3 * QB] + p[3 * QB:4 * QB]
        pg_hi, pg_lo = _split(pg)
        imp_t = _dot_nt(ovt_ref[...], pg_hi) + _dot_nt(ovt_ref[...], pg_lo)
        score = jnp.where(blk_valid, jnp.where(blk_forced, FORCE_SCORE, imp_t), -1.0)
        sel_t = _topk_mask(score, jb, ns, n_top)
        if ns < LANE:
            sel_t = jnp.concatenate([sel_t, jnp.zeros((LANE - ns, QB), F32)], axis=0)
        sel = sel_t.T.astype(BF16)

        def kv_step(kt, carry, qr=qr, sel=sel):
            m, l, acc = carry
            k0 = pl.multiple_of(kt * KT, KT)
            kb = ks_ref[pl.ds(k0, KT), :].astype(BF16)
            vb = vs_ref[pl.ds(k0, KT), :].astype(BF16)
            s = _dot_nt(qr, kb)
            kpos = k0 + lax.broadcasted_iota(jnp.int32, (1, KT), 1)
            mk = (_dot(sel, e_ref[kt]) > 0.5) & (kpos <= qpos)
            s3 = jnp.where(mk[None], s.reshape(HPG, QB, KT), NEG)
            m_new = jnp.maximum(m, jnp.max(s3, axis=-1, keepdims=True))
            alpha = jnp.exp(m - m_new)
            pe = jnp.exp(s3 - m_new)
            l = alpha * l + jnp.sum(pe, axis=-1, keepdims=True)
            pv = _dot(pe.reshape(HPG * QB, KT).astype(BF16), vb)
            return m_new, l, alpha * acc + pv.reshape(HPG, QB, LANE)

        init = (jnp.full((HPG, QB, 1), NEG, F32), jnp.zeros((HPG, QB, 1), F32), jnp.zeros((HPG, QB, LANE), F32))
        _, l, acc = lax.fori_loop(0, n_kt, kv_step, init)
        o_sel = (acc / l).reshape(HPG * QB, LANE)
        pw = _masked_softmax(_dot_nt(qr, kwb), win_valid)
        o_win = _dot(pw.astype(BF16), vwb)

        for jj in range(2):
            def place(o, jj=jj, g=g):
                a = o[(2 * jj) * QB:(2 * jj + 1) * QB]
                b = o[(2 * jj + 1) * QB:(2 * jj + 2) * QB]
                if g == 0:
                    b = pltpu.roll(b, HEAD_DIM, 1)
                else:
                    a = pltpu.roll(a, HEAD_DIM, 1)
                return jnp.where(low, a, b)
            j = 2 * g + jj
            o_ref[:, j * LANE:(j + 1) * LANE] = (
                gexp[:, j * LANE:(j + 1) * LANE] * place(o_cmp)
                + gexp[:, A_WIDTH + j * LANE:A_WIDTH + (j + 1) * LANE] * place(o_sel)
                + gexp[:, 2 * A_WIDTH + j * LANE:2 * A_WIDTH + (j + 1) * LANE] * place(o_win))


def _nsa_prompt(yq, gt, abk, abv, ks, vs, kw, vw, ovt, emat, ge, batch, seq):
    nq = seq // QB
    nc = seq // CMP_STRIDE
    qrow = lambda wd: pl.BlockSpec((QB, wd), lambda b, i: (b * nq + i, 0))
    per_b = lambda r, wd: pl.BlockSpec((r, wd), lambda b, i: (b, 0))
    return pl.pallas_call(
        functools.partial(_nsa_prompt_body, seq=seq), grid=(batch, nq),
        in_specs=[qrow(2 * A_WIDTH), qrow(LANE), per_b(nc, 256), per_b(nc, 256), per_b(seq, LANE), per_b(seq, LANE),
                  per_b(seq, LANE), per_b(seq, LANE), _full(ovt.shape), _full(emat.shape), _full(ge.shape)],
        out_specs=qrow(A_WIDTH), out_shape=jax.ShapeDtypeStruct((batch * seq, A_WIDTH), F32),
        compiler_params=_params(("parallel", "arbitrary"), VMEM_BIG), name="nsa_prompt")(
            yq, gt, abk, abv, ks, vs, kw, vw, ovt, emat, ge)


def _nsa_decode_body(pt_ref, q2_ref, gt_ref, ksn_ref, vsn_ref, kwn_ref, vwn_ref, wink_ref, winv_ref, ovt_ref,
                     e_ref, *rest, n_pages, page):
    del pt_ref
    abk_refs = rest[0:n_pages]
    abv_refs = rest[n_pages:2 * n_pages]
    sk_refs = rest[2 * n_pages:3 * n_pages]
    sv_refs = rest[3 * n_pages:4 * n_pages]
    o_ref = rest[4 * n_pages]
    past = n_pages * page
    ncd = past // CMP_STRIDE
    nsd = past // SEL_BLOCK + 1
    nsp = -(-nsd // 8) * 8
    n_top = min(SEL_TOP, nsd)
    cur = past // SEL_BLOCK
    scale = HEAD_DIM ** -0.5
    q2 = q2_ref[...] * scale
    qc = q2[0:A_HEADS].astype(BF16)
    qr_f = q2[A_HEADS:2 * A_HEADS]
    qr = qr_f.astype(BF16)
    rowi = lax.broadcasted_iota(jnp.int32, (A_HEADS, 1), 0)

    abk = jnp.concatenate([r[...] for r in abk_refs], axis=0)
    abv = jnp.concatenate([r[...] for r in abv_refs], axis=0)
    kcf = (abk[:, :LANE] + pltpu.roll(abk[:, LANE:], ncd - 1, 0)).astype(BF16)
    vcf = (abv[:, :LANE] + pltpu.roll(abv[:, LANE:], ncd - 1, 0)).astype(BF16)
    ncol = lax.broadcasted_iota(jnp.int32, (A_HEADS, ncd), 1)
    p = _masked_softmax(_dot_nt(qc, kcf), ncol < ncd - 1)
    o_cmp = _dot(p.astype(BF16), vcf)
    pg0 = jnp.sum(p[0:HPG], axis=0, keepdims=True)
    pg1 = jnp.sum(p[HPG:2 * HPG], axis=0, keepdims=True)
    rows128 = lax.broadcasted_iota(jnp.int32, (LANE, 1), 0)
    pg = jnp.where(rows128 == 0, pg0, jnp.where(rows128 == 1, pg1, 0.0))
    pg_hi, pg_lo = _split(pg)
    imp_t = _dot_nt(ovt_ref[...], pg_hi) + _dot_nt(ovt_ref[...], pg_lo)
    jb = lax.broadcasted_iota(jnp.int32, (nsp, LANE), 0)
    blk_forced = (jb == 0) | (jb == cur) | (jb == cur - 1)
    score = jnp.where(jb <= cur, jnp.where(blk_forced, FORCE_SCORE, imp_t), -1.0)
    sel_t = _topk_mask(score, jb, nsd, n_top)
    sel_t = jnp.concatenate([sel_t, jnp.zeros((LANE - nsp, LANE), F32)], axis=0)
    sel = sel_t.T
    selh = jnp.where(rowi < HPG, sel[0:1], sel[1:2]).astype(BF16)
    mask = _dot(selh, e_ref[...]) > 0.5

    s = jnp.concatenate([_dot(qr, r[...].astype(BF16)) for r in sk_refs], axis=1)
    s = jnp.where(mask, s, NEG)
    s_new = jnp.sum(qr_f * ksn_ref[...], axis=1, keepdims=True)
    m = jnp.maximum(jnp.max(s, axis=1, keepdims=True), s_new)
    e = jnp.exp(s - m)
    e_new = jnp.exp(s_new - m)
    d = jnp.sum(e, axis=1, keepdims=True) + e_new
    acc = e_new * vsn_ref[...]
    for j, r in enumerate(sv_refs):
        acc = acc + _dot_nt(e[:, j * page:(j + 1) * page].astype(BF16), r[...].astype(BF16))
    o_sel = acc / d

    s = _dot(qr, wink_ref[...].astype(BF16))
    s_new = jnp.sum(qr_f * kwn_ref[...], axis=1, keepdims=True)
    m = jnp.maximum(jnp.max(s, axis=1, keepdims=True), s_new)
    e = jnp.exp(s - m)
    e_new = jnp.exp(s_new - m)
    d = jnp.sum(e, axis=1, keepdims=True) + e_new
    o_win = (_dot_nt(e.astype(BF16), winv_ref[...].astype(BF16)) + e_new * vwn_ref[...]) / d

    sg = jax.nn.sigmoid(gt_ref[...])
    lane8 = lax.broadcasted_iota(jnp.int32, (A_HEADS, LANE), 1)
    gcol = lambda c: jnp.sum(jnp.where(lane8 == rowi * 3 + c, sg, 0.0), axis=1, keepdims=True)
    o8 = gcol(0) * o_cmp + gcol(1) * o_sel + gcol(2) * o_win
    low = lax.broadcasted_iota(jnp.int32, (1, LANE), 1) < HEAD_DIM
    for j in range(A_HEADS // 2):
        a, b = o8[2 * j:2 * j + 1], o8[2 * j + 1:2 * j + 2]
        if (2 * j) // HPG == 0:
            b = pltpu.roll(b, HEAD_DIM, 1)
        else:
            a = pltpu.roll(a, HEAD_DIM, 1)
        o_ref[:, j * LANE:(j + 1) * LANE] = jnp.where(low, a, b)


def _nsa_decode(page_table, q2, gt, ksn, vsn, kwn, vwn, wink, winv, ovt, emat, abk_pool, abv_pool, selk, selv,
                layer):
    nb, n_pages = page_table.shape
    page = selk.shape[3]
    row3 = lambda wd: pl.BlockSpec((None, 1, wd), lambda b, pt: (b, 0, 0))
    win = pl.BlockSpec((None, None, LANE, wink.shape[3]), lambda b, pt: (layer, b, 0, 0))
    ab_specs = [pl.BlockSpec((None, page // CMP_STRIDE, 256), lambda b, pt, j=j: (pt[b * n_pages + j], 0, 0))
                for j in range(n_pages)]
    pg_specs = [pl.BlockSpec((None, None, LANE, page), lambda b, pt, j=j: (layer, pt[b * n_pages + j], 0, 0))
                for j in range(n_pages)]
    grid_spec = pltpu.PrefetchScalarGridSpec(
        num_scalar_prefetch=1, grid=(nb,),
        in_specs=[pl.BlockSpec((None, 2 * A_HEADS, LANE), lambda b, pt: (b, 0, 0)), row3(LANE), row3(LANE),
                  row3(LANE), row3(LANE), row3(LANE), win, win,
                  pl.BlockSpec(ovt.shape, lambda b, pt: (0, 0)), pl.BlockSpec(emat.shape, lambda b, pt: (0, 0))]
        + ab_specs + ab_specs + pg_specs + pg_specs,
        out_specs=pl.BlockSpec((None, 1, A_WIDTH), lambda b, pt: (b, 0, 0)))
    return pl.pallas_call(
        functools.partial(_nsa_decode_body, n_pages=n_pages, page=page), grid_spec=grid_spec,
        out_shape=jax.ShapeDtypeStruct((nb, 1, A_WIDTH), F32),
        compiler_params=_params(("arbitrary",), VMEM_BIG), name="nsa_decode")(
            page_table.reshape(-1), q2, gt, ksn, vsn, kwn, vwn, wink, winv, ovt, emat,
            *([abk_pool] * n_pages), *([abv_pool] * n_pages), *([selk] * n_pages), *([selv] * n_pages))


def _conv_tail(c, lng_ref, lnb_ref, pw_ref, pwb_ref):
    mu = jnp.mean(c, axis=-1, keepdims=True)
    cen = c - mu
    var = jnp.mean(cen * cen, axis=-1, keepdims=True)
    y = cen * lax.rsqrt(var + LN_EPS) * lng_ref[...] + lnb_ref[...]
    y = y * jax.nn.sigmoid(y)
    return _dot(y.astype(BF16), pw_ref[...]) + pwb_ref[...]


def _conv_prompt_body(x_ref, cw_ref, cb_ref, lng_ref, lnb_ref, pw_ref, pwb_ref, o_ref, st_ref, u_ref, *, tt):
    halo = 32
    i = pl.program_id(1)

    @pl.when(i == 0)
    def _():
        u_ref[0:halo, :] = jnp.zeros((halo, B_WIDTH), F32)

    x = x_ref[...]
    u_ref[halo:halo + tt, :] = x[:, 0:B_WIDTH] * jax.nn.sigmoid(x[:, B_WIDTH:2 * B_WIDTH])
    acc = jnp.zeros((tt, B_WIDTH), F32) + cb_ref[...]
    for w in range(CONV_W):
        off = halo - (CONV_W - 1) + w
        acc = acc + u_ref[off:off + tt, :] * cw_ref[w:w + 1, :]
    o_ref[...] = _conv_tail(acc, lng_ref, lnb_ref, pw_ref, pwb_ref)
    last = u_ref[tt:tt + halo, :]
    st_ref[...] = last
    u_ref[0:halo, :] = last


def _conv_prompt(cvcg, cw, cb, lng, lnb, pw, pwb, batch, seq):
    tt = _row_tile(seq, 512)
    nt = seq // tt
    vec = _full((1, B_WIDTH))
    return pl.pallas_call(
        functools.partial(_conv_prompt_body, tt=tt), grid=(batch, nt),
        in_specs=[pl.BlockSpec((tt, 2 * B_WIDTH), lambda b, i: (b * nt + i, 0)), _full((32, B_WIDTH)), vec, vec, vec,
                  _full((B_WIDTH, B_WIDTH)), vec],
        out_specs=[pl.BlockSpec((tt, B_WIDTH), lambda b, i: (b * nt + i, 0)),
                   pl.BlockSpec((None, 32, B_WIDTH), lambda b, i: (b, 0, 0))],
        out_shape=[jax.ShapeDtypeStruct((batch * seq, B_WIDTH), F32),
                   jax.ShapeDtypeStruct((batch, 32, B_WIDTH), F32)],
        scratch_shapes=[pltpu.VMEM((tt + 32, B_WIDTH), F32)],
        compiler_params=_params(("parallel", "arbitrary")), name="conv_prompt")(cvcg, cw, cb, lng, lnb, pw, pwb)


def _conv_sample_body(x_ref, st_ref, cw_ref, cb_ref, lng_ref, lnb_ref, pw_ref, pwb_ref, o_ref, new_ref):
    x = x_ref[...]
    u = x[:, 0:B_WIDTH] * jax.nn.sigmoid(x[:, B_WIDTH:2 * B_WIDTH])
    acc = u * cw_ref[CONV_W - 1:CONV_W, :] + cb_ref[...]
    for w in range(CONV_W - 1):
        row = st_ref[:, w, :]
        acc = acc + row * cw_ref[w:w + 1, :]
        if w > 0:
            new_ref[:, w - 1, :] = row
    new_ref[:, CONV_W - 2, :] = u
    o_ref[...] = _conv_tail(acc, lng_ref, lnb_ref, pw_ref, pwb_ref)


def _conv_sample(cvcg, state, layer, cw, cb, lng, lnb, pw, pwb):
    nb = cvcg.shape[0]
    vec = _full((1, B_WIDTH))
    return pl.pallas_call(
        _conv_sample_body, grid=(1,),
        in_specs=[_full((nb, 2 * B_WIDTH)),
                  pl.BlockSpec((None, nb, CONV_W - 1, B_WIDTH), lambda i: (layer, 0, 0, 0)),
                  _full((32, B_WIDTH)), vec, vec, vec, _full((B_WIDTH, B_WIDTH)), vec],
        out_specs=[_full((nb, B_WIDTH)), _full((nb, CONV_W - 1, B_WIDTH))],
        out_shape=[jax.ShapeDtypeStruct((nb, B_WIDTH), F32),
                   jax.ShapeDtypeStruct((nb, CONV_W - 1, B_WIDTH), F32)],
        compiler_params=_params(("arbitrary",)), name="conv_sample")(cvcg, state, cw, cb, lng, lnb, pw, pwb)


def _softplus(x):
    return jnp.maximum(x, 0.0) + jnp.log(1.0 + jnp.exp(-jnp.abs(x)))


def _rwkv_pre(rw, prev, k_next, mu_ref, w0_ref, w2_ref, a0_ref, a2_ref, g2_ref, kk_ref, ka_ref, rk_ref, seg_ref):
    xm = rw + (prev - rw) * mu_ref[...]
    r, k, v, lor = xm[:, 0:256], xm[:, 256:512], xm[:, 512:768], xm[:, 768:896]
    seg = lambda x: _dot2(x, seg_ref[...])
    w_log = -_softplus(-(w0_ref[...] + _dot(jnp.tanh(lor).astype(BF16), w2_ref[...]))) - 0.5
    decay = jnp.exp(-jnp.exp(w_log))
    a = jax.nn.sigmoid(a0_ref[...] + _dot(lor.astype(BF16), a2_ref[...]))
    gate = _dot(jax.nn.sigmoid(lor).astype(BF16), g2_ref[...])
    unit = lambda x: x / jnp.maximum(jnp.sqrt(seg(x * x)), 1e-12)
    kk = unit(k * kk_ref[...])
    kmod = k * (1.0 + (a - 1.0) * ka_ref[...])
    bb = kk * a
    out = dict(nk=-kk, w=decay, b=bb, k=kmod, v=v, gate=gate,
               rp=decay * r - kk * seg(bb * r),
               vc=v * seg(kmod * r), vb=v * seg(r * kmod * rk_ref[...]))
    if k_next is not None:
        mu_k = mu_ref[:, 256:512]
        kk_n = unit((k_next + (rw[:, 256:512] - k_next) * mu_k) * kk_ref[...])
        out.update(wk=-decay * kk_n, beta=-seg(bb * kk_n), gamma=-seg(kmod * kk_n))
    return out


_PROMPT_OUTS = ("w", "b", "k", "wk", "beta", "gamma", "rp", "v", "vc", "vb", "gate")
_SAMPLE_OUTS = ("nk", "w", "b", "k", "rp", "v", "vc", "vb", "gate")


def _rwkv_prep_prompt_body(rw_ref, kn_ref, *refs, tt):
    n_out = len(_PROMPT_OUTS)
    params, outs, carry_ref = refs[:10], refs[10:10 + n_out], refs[10 + n_out]

    @pl.when(pl.program_id(1) == 0)
    def _():
        carry_ref[...] = jnp.zeros(carry_ref.shape, F32)

    rw = rw_ref[...]
    rowi = lax.broadcasted_iota(jnp.int32, (tt, 1), 0)
    prev = jnp.where(rowi == 0, carry_ref[0:1, :], pltpu.roll(rw, 1, 0))
    carry_ref[0:1, :] = rw[tt - 1:tt, :]
    vals = _rwkv_pre(rw, prev, kn_ref[...], *params)
    for name, ref in zip(_PROMPT_OUTS, outs):
        ref[...] = vals[name]


def _rwkv_prep_sample_body(rw_ref, prev_ref, *refs):
    vals = _rwkv_pre(rw_ref[...], prev_ref[...], None, *refs[:10])
    for name, ref in zip(_SAMPLE_OUTS, refs[10:]):
        ref[...] = vals[name]


def _rwkv_param_specs():
    vec = lambda wd: _full((1, wd))
    return [vec(RW_COLS), vec(C_WIDTH), _full((LANE, C_WIDTH)), vec(C_WIDTH), _full((LANE, C_WIDTH)),
            _full((LANE, C_WIDTH)), vec(C_WIDTH), vec(C_WIDTH), vec(C_WIDTH), _full((C_WIDTH, C_WIDTH))]


def _rwkv_prep_prompt(rw, rw_next, params, batch, seq):
    tt = _row_tile(seq, 512)
    nt = seq // tt
    row = lambda wd: pl.BlockSpec((tt, wd), lambda b, i: (b * nt + i, 0))
    n_out = len(_PROMPT_OUTS)
    return pl.pallas_call(
        functools.partial(_rwkv_prep_prompt_body, tt=tt), grid=(batch, nt),
        in_specs=[row(RW_COLS), pl.BlockSpec((tt, C_WIDTH), lambda b, i: (b * nt + i, 1))] + _rwkv_param_specs(),
        out_specs=[row(C_WIDTH)] * n_out,
        out_shape=[jax.ShapeDtypeStruct((batch * seq, C_WIDTH), F32)] * n_out,
        scratch_shapes=[pltpu.VMEM((8, RW_COLS), F32)],
        compiler_params=_params(("parallel", "arbitrary")), name="rwkv_prep_prompt")(rw, rw_next, *params)


def _rwkv_prep_sample(rw, prev, layer, params):
    nb = rw.shape[0]
    return pl.pallas_call(
        _rwkv_prep_sample_body, grid=(1,),
        in_specs=[_full((nb, RW_COLS)), pl.BlockSpec((None, nb, RW_COLS), lambda i: (layer, 0, 0))]
        + _rwkv_param_specs(),
        out_specs=[_full((nb, C_WIDTH))] * 9,
        out_shape=[jax.ShapeDtypeStruct((nb, C_WIDTH), F32)] * 9,
        compiler_params=_params(("arbitrary",)), name="rwkv_prep_sample")(rw, prev, *params)


def _wkv_head_step(s, nk, w, bb, kk, rp, vtile, ones):
    sk = _dot((s * nk).astype(BF16), ones)
    o8 = _dot_nt(jnp.broadcast_to(rp, (8, HEAD_DIM)).astype(BF16), s.astype(BF16))
    return s * w + sk * bb + vtile * kk, o8[0:1]


def _wkv_scan_body(w_ref, b_ref, k_ref, wk_ref, be_ref, ga_ref, rp_ref, v_ref, seg_ref, o_ref, sfin_ref, s_ref,
                   sk_ref, vt_ref, *, batch, tb):
    step_id = pl.program_id(0)
    n_pair = C_HEADS // 2

    @pl.when(step_id == 0)
    def _():
        s_ref[...] = jnp.zeros(s_ref.shape, F32)
        sk_ref[...] = jnp.zeros(sk_ref.shape, F32)

    for b in range(batch):
        hi, lo = _split(v_ref[b].T)
        vt_ref[b, :, 0:tb] = hi
        vt_ref[b, :, tb:2 * tb] = lo
    rowi = lax.broadcasted_iota(jnp.int32, (2 * tb, LANE), 0)
    low = lax.broadcasted_iota(jnp.int32, (HEAD_DIM, LANE), 1) < HEAD_DIM
    low1 = low[0:1]
    row8 = lax.broadcasted_iota(jnp.int32, (8, LANE), 0)
    lane8 = lax.broadcasted_iota(jnp.int32, (8, LANE), 1)
    rq_mask = ((row8 == 0) & (lane8 < HEAD_DIM)) | ((row8 == 1) & (lane8 >= HEAD_DIM))
    seg = seg_ref[...]

    def step(t, carry):
        onehot = jnp.where((rowi == t) | (rowi == t + tb), 1.0, 0.0).astype(BF16)
        for b in range(batch):
            vcols = _dot(vt_ref[b], onehot)
            rows = [ref[b, pl.ds(t, 1), :] for ref in (w_ref, b_ref, k_ref, wk_ref, be_ref, ga_ref, rp_ref)]
            o_rows = []
            for p in range(n_pair):
                w, bb, kk, wk, be, ga, rp = [x[:, p * LANE:(p + 1) * LANE] for x in rows]
                s, sk = s_ref[b, p], sk_ref[b, p]
                m = _dot((s * wk).astype(BF16), seg)
                s16 = s.astype(BF16)
                o8 = _dot_nt(jnp.where(rq_mask, rp, 0.0).astype(BF16), jnp.concatenate([s16, s16], axis=0))
                o_rows.append(jnp.where(low1, o8[0:1], o8[1:2]))
                vtile = jnp.where(low, vcols[p * LANE:p * LANE + HEAD_DIM], vcols[p * LANE + HEAD_DIM:(p + 1) * LANE])
                s_ref[b, p] = s * w + sk * bb + vtile * kk
                sk_ref[b, p] = m + sk * be + vtile * ga
            o_ref[b, pl.ds(t, 1), :] = jnp.concatenate(o_rows, axis=1)
        return carry

    lax.fori_loop(0, tb, step, 0)

    @pl.when(step_id == pl.num_programs(0) - 1)
    def _():
        for b in range(batch):
            for p in range(n_pair):
                s = s_ref[b, p]
                sfin_ref[b, 2 * p] = s[:, 0:HEAD_DIM]
                sfin_ref[b, 2 * p + 1] = pltpu.roll(s, HEAD_DIM, 1)[:, 0:HEAD_DIM]


def _wkv_scan(w, bb, kk, wk, beta, gamma, rp, v, seg_pair, batch, seq):
    tb = _row_tile(seq, 128)
    blk = pl.BlockSpec((batch, tb, C_WIDTH), lambda i: (0, i, 0))
    r3 = lambda x: x.reshape(batch, seq, C_WIDTH)
    return pl.pallas_call(
        functools.partial(_wkv_scan_body, batch=batch, tb=tb), grid=(seq // tb,),
        in_specs=[blk] * 8 + [_full((LANE, LANE))],
        out_specs=[blk, _full((batch, C_HEADS, HEAD_DIM, HEAD_DIM))],
        out_shape=[jax.ShapeDtypeStruct((batch, seq, C_WIDTH), F32),
                   jax.ShapeDtypeStruct((batch, C_HEADS, HEAD_DIM, HEAD_DIM), F32)],
        scratch_shapes=[pltpu.VMEM((batch, C_HEADS // 2, HEAD_DIM, LANE), F32),
                        pltpu.VMEM((batch, C_HEADS // 2, HEAD_DIM, LANE), F32),
                        pltpu.VMEM((batch, C_WIDTH, 2 * tb), BF16)],
        compiler_params=_params(("arbitrary",)), name="wkv_scan")(
            r3(w), r3(bb), r3(kk), r3(wk), r3(beta), r3(gamma), r3(rp), r3(v), seg_pair)


def _wkv_sample_body(nk_ref, w_ref, b_ref, k_ref, rp_ref, v_ref, ones_ref, st_ref, o_ref, new_ref, vth_ref, vtl_ref,
                     *, nb, bs):
    base = pl.program_id(0) * bs
    hi, lo = _split(v_ref[...].T)
    vth_ref[...] = hi
    vtl_ref[...] = lo
    rowi = lax.broadcasted_iota(jnp.int32, (nb, HEAD_DIM), 0)
    ones = ones_ref[...]

    def step(j, carry):
        t = base + j
        onehot = jnp.where(rowi == t, 1.0, 0.0).astype(BF16)
        vcols = _dot(vth_ref[...], onehot) + _dot(vtl_ref[...], onehot)
        nk, w, bb = nk_ref[pl.ds(t, 1), :], w_ref[pl.ds(t, 1), :], b_ref[pl.ds(t, 1), :]
        kk, rp = k_ref[pl.ds(t, 1), :], rp_ref[pl.ds(t, 1), :]
        for h in range(C_HEADS):
            sl = slice(h * HEAD_DIM, (h + 1) * HEAD_DIM)
            s_new, o_row = _wkv_head_step(st_ref[j, h], nk[:, sl], w[:, sl], bb[:, sl], kk[:, sl], rp[:, sl],
                                          vcols[sl, :], ones)
            new_ref[j, h] = s_new
            o_ref[h, pl.ds(t, 1), :] = o_row
        return carry

    lax.fori_loop(0, bs, step, 0)


def _wkv_sample(nk, w, bb, kk, rp, v, ones, state, layer):
    nb = nk.shape[0]
    bs = _row_tile(nb, 32)
    vec = _full((nb, C_WIDTH))
    return pl.pallas_call(
        functools.partial(_wkv_sample_body, nb=nb, bs=bs), grid=(nb // bs,),
        in_specs=[vec] * 6 + [_full((HEAD_DIM, HEAD_DIM)),
                              pl.BlockSpec((None, bs, C_HEADS, HEAD_DIM, HEAD_DIM), lambda i: (layer, i, 0, 0, 0))],
        out_specs=[_full((C_HEADS, nb, HEAD_DIM)),
                   pl.BlockSpec((bs, C_HEADS, HEAD_DIM, HEAD_DIM), lambda i: (i, 0, 0, 0))],
        out_shape=[jax.ShapeDtypeStruct((C_HEADS, nb, HEAD_DIM), F32),
                   jax.ShapeDtypeStruct((nb, C_HEADS, HEAD_DIM, HEAD_DIM), F32)],
        scratch_shapes=[pltpu.VMEM((C_WIDTH, nb), BF16), pltpu.VMEM((C_WIDTH, nb), BF16)],
        compiler_params=_params(("arbitrary",)), name="wkv_sample")(nk, w, bb, kk, rp, v, ones, state)


def _merge_body(h_ref, oa_ref, ob_ref, os_ref, vc_ref, vb_ref, gate_ref, lnw_ref, lnb_ref, gn_ref, n3_ref, wout_ref,
                seg_ref, o_ref):
    seg = lambda x: _dot2(x, seg_ref[...])
    o = os_ref[...] + vc_ref[...]
    cen = o - seg(o) * (1.0 / HEAD_DIM)
    var = seg(cen * cen) * (1.0 / HEAD_DIM)
    o = cen * lax.rsqrt(var + GN_EPS) * lnw_ref[...] + lnb_ref[...]
    o = (o + vb_ref[...]) * gate_ref[...]
    gn = gn_ref[...]
    merged = jnp.concatenate([_rms(oa_ref[...], gn[:, 0:512]), _rms(ob_ref[...], gn[:, 512:768]),
                              _rms(o, gn[:, 768:1024])], axis=1).astype(BF16)
    o_ref[...] = h_ref[...] + _rms(_dot(merged, wout_ref[...]), n3_ref[...])


def _merge(h, oa, ob, os_, vc, vb, gate, lnw, lnb, gn, n3, wout, seg):
    n = h.shape[0]
    tm = _row_tile(n, 512)
    row = lambda wd: pl.BlockSpec((tm, wd), lambda i: (i, 0))
    vec = lambda wd: _full((1, wd))
    return pl.pallas_call(
        _merge_body, grid=(n // tm,),
        in_specs=[row(D_MODEL), row(A_WIDTH), row(B_WIDTH), row(C_WIDTH), row(C_WIDTH), row(C_WIDTH), row(C_WIDTH),
                  vec(C_WIDTH), vec(C_WIDTH), vec(D_MODEL), vec(D_MODEL), _full((D_MODEL, D_MODEL)),
                  _full((C_WIDTH, C_WIDTH))],
        out_specs=row(D_MODEL), out_shape=jax.ShapeDtypeStruct((n, D_MODEL), F32),
        compiler_params=_params(("parallel",), VMEM_BIG), name="merge")(
            h, oa, ob, os_, vc, vb, gate, lnw, lnb, gn, n3, wout, seg)


def _rope_tables(pos):
    half = ROT_DIM // 2
    inv = ROPE_THETA ** (-jnp.arange(half, dtype=F32) * 2.0 / ROT_DIM)
    ang = pos.astype(F32)[:, None] * inv[None, :]
    cos, sin = jnp.cos(ang), jnp.sin(ang)
    n = pos.shape[0]
    one = jnp.ones((n, HEAD_DIM - ROT_DIM), F32)
    zero = jnp.zeros((n, HEAD_DIM - ROT_DIM), F32)
    zh = jnp.zeros((n, half), F32)
    c = jnp.concatenate([cos, cos, one], axis=1)
    sa = jnp.concatenate([zh, sin, zero], axis=1)
    sb = jnp.concatenate([-sin, zh, zero], axis=1)
    tile = lambda x: jnp.concatenate([x, x], axis=1)
    return tile(c), tile(sa), tile(sb)


def _overlap_t(n_cmp, n_sel, rows, cols):
    ci = np.arange(n_cmp)[None, :] * CMP_STRIDE
    sj = np.arange(n_sel)[:, None] * SEL_BLOCK
    ov = ((ci < sj + SEL_BLOCK) & (ci + CMP_BLOCK > sj)).astype(np.float32)
    out = np.zeros((rows, cols), np.float32)
    out[:n_sel, :n_cmp] = ov
    return jnp.asarray(out, BF16)


def _expand_tiles(seq):
    nt = -(-seq // KT)
    key = np.arange(nt * KT).reshape(nt, 1, KT)
    blk = np.arange(LANE).reshape(1, LANE, 1)
    return jnp.asarray((key // SEL_BLOCK == blk).astype(np.float32), BF16)


def _gate_expand():
    ge = np.zeros((LANE, 3 * A_WIDTH), np.float32)
    for h in range(A_HEADS):
        for c in range(3):
            ge[h * 3 + c, c * A_WIDTH + h * HEAD_DIM:c * A_WIDTH + (h + 1) * HEAD_DIM] = 1.0
    return jnp.asarray(ge, BF16)


def _seg_ones():
    idx = np.arange(C_WIDTH) // HEAD_DIM
    return jnp.asarray((idx[:, None] == idx[None, :]).astype(np.float32), BF16)


def _cmp_weights(cmp_w_l, cmp_pe_l):
    eye = jnp.eye(A_KV_HEADS, dtype=F32)
    outs = []
    for i in range(2):
        w4 = cmp_w_l[i].reshape(2, CMP_STRIDE, HEAD_DIM, HEAD_DIM)
        wbig = jnp.einsum('hlde,gk->lgdhke', w4, eye).reshape(CMP_STRIDE * KV_WIDTH, 2 * KV_WIDTH)
        pe = cmp_pe_l[i].reshape(2, CMP_STRIDE, 1, HEAD_DIM)
        pe2 = jnp.broadcast_to(pe, (2, CMP_STRIDE, A_KV_HEADS, HEAD_DIM)).reshape(2, CMP_STRIDE * KV_WIDTH)
        outs.append((pe2, wbig.astype(BF16)))
    return outs


def _lora_pad(w, start):
    return jnp.zeros((LANE, C_WIDTH), F32).at[start:start + w.shape[0]].set(w).astype(BF16)


def kernel(x_prompt, x_sample, cache_cmp_k, cache_cmp_v, cache_sel_k, cache_sel_v, cache_win_k, cache_win_v, state_conv, state_shift, state_wkv, page_table, w_in, w_out, norms, grp_norm, ffn_gate, ffn_up, ffn_down, cmp_pe, cmp_w, conv_w, conv_b, conv_ln_g, conv_ln_b, conv_pw, conv_pw_b, rw_mu, rw_w0, rw_w2, rw_a0, rw_a2, rw_g2, rw_kk, rw_ka, rw_rk, rw_ln_w, rw_ln_b):
    batch, seq, _ = x_prompt.shape
    nb = x_sample.shape[0]
    depth = w_in.shape[0]
    n_pool, page = cache_cmp_k.shape[1], cache_cmp_k.shape[2]
    n_pages = page_table.shape[1]
    past = n_pages * page
    win_buf = cache_win_k.shape[2]
    n_p = batch * seq

    rope_p = _rope_tables(jnp.tile(jnp.arange(seq), batch))
    rope_s = _rope_tables(jnp.full((nb,), past))
    nc, ns = seq // CMP_STRIDE, seq // SEL_BLOCK
    ovt_p = _overlap_t(nc - 1, ns, ns, nc)
    emat_p = _expand_tiles(seq)
    ncd, nsd = past // CMP_STRIDE, past // SEL_BLOCK + 1
    ovt_s = _overlap_t(ncd - 1, nsd, -(-nsd // 8) * 8, ncd)
    emat_s = jnp.asarray((np.arange(past)[None, :] // SEL_BLOCK == np.arange(LANE)[:, None]).astype(np.float32), BF16)
    ge = _gate_expand()
    seg = _seg_ones()
    ones64 = jnp.ones((HEAD_DIM, HEAD_DIM), BF16)

    tview = lambda c: jnp.transpose(c, (0, 1, 3, 4, 2)).reshape(c.shape[0], c.shape[1], KV_WIDTH, c.shape[2])
    selk, selv, wink, winv = tview(cache_sel_k), tview(cache_sel_v), tview(cache_win_k), tview(cache_win_v)
    pool_k, pool_v = tview(cache_cmp_k), tview(cache_cmp_v)

    hp = x_prompt.reshape(n_p, D_MODEL)
    hs = x_sample.reshape(nb, D_MODEL)
    sp = {k: [] for k in ('cmp_k', 'cmp_v', 'sel_k', 'sel_v', 'win_k', 'win_v', 'conv', 'shift', 'wkv')}
    ss = {k: [] for k in sp}
    vec = lambda x: x.reshape(1, -1)
    grp = jnp.arange(A_HEADS) // HPG

    for l in range(depth):
        nr = norms[l]
        wg, wu, wd = ffn_gate[l].astype(BF16), ffn_up[l].astype(BF16), ffn_down[l].astype(BF16)
        w_in_l = w_in[l]
        w_in_p = jnp.concatenate([w_in_l[:, :1304], jnp.zeros((D_MODEL, IN_PAD - w_in_l.shape[1]), F32),
                                  w_in_l[:, 1304:]], axis=1).astype(BF16)
        w_out_l = w_out[l].astype(BF16)
        (pe_k, wb_k), (pe_v, wb_v) = _cmp_weights(cmp_w[l], cmp_pe[l])
        cw = jnp.zeros((32, B_WIDTH), F32).at[:CONV_W].set(conv_w[l])
        conv_params = (cw, vec(conv_b[l]), vec(conv_ln_g[l]), vec(conv_ln_b[l]), conv_pw[l].astype(BF16),
                       vec(conv_pw_b[l]))
        rw_params = (vec(rw_mu[l]), vec(rw_w0[l]), _lora_pad(rw_w2[l], 0), vec(rw_a0[l]),
                     _lora_pad(rw_a2[l], LORA_W), _lora_pad(rw_g2[l], LORA_W + LORA_A), vec(rw_kk[l]),
                     vec(rw_ka[l]), vec(rw_rk[l]), seg)
        merge_params = (vec(rw_ln_w[l]), vec(rw_ln_b[l]), vec(grp_norm[l]), vec(nr[3]), w_out_l, seg)

        hp = _ffn(hp, vec(nr[0]), vec(nr[1]), wg[0], wu[0], wd[0])
        yq, kc, vc, ks, vs, kw, vw, gt, cvcg, rw = _proj(hp, vec(nr[2]), w_in_p, *rope_p)
        abk = _compress(kc.reshape(1, n_p // CMP_STRIDE, CMP_STRIDE * KV_WIDTH), 0, pe_k, wb_k)
        abv = _compress(vc.reshape(1, n_p // CMP_STRIDE, CMP_STRIDE * KV_WIDTH), 0, pe_v, wb_v)
        o_a = _nsa_prompt(yq, gt, abk, abv, ks, vs, kw, vw, ovt_p, emat_p, ge, batch, seq)
        o_b, conv_st = _conv_prompt(cvcg, *conv_params, batch, seq)
        rw_next = jnp.concatenate([rw[1:], jnp.zeros((1, RW_COLS), F32)], axis=0)
        wdec, bb, kmod, wk, beta, gamma, rp, v, vcc, vbb, gate = _rwkv_prep_prompt(rw, rw_next, rw_params, batch, seq)
        o_s, s_fin = _wkv_scan(wdec, bb, kmod, wk, beta, gamma, rp, v, seg[:LANE, :LANE], batch, seq)
        o_s = o_s.reshape(n_p, C_WIDTH)
        hp = _merge(hp, o_a, o_b, o_s, vcc, vbb, gate, *merge_params)
        hp = _ffn(hp, vec(nr[4]), vec(nr[5]), wg[1], wu[1], wd[1])
        kv5 = lambda x: x.reshape(batch, seq, A_KV_HEADS, HEAD_DIM)
        n_buf = min(WINDOW, seq)
        sp['cmp_k'].append(kv5(kc)); sp['cmp_v'].append(kv5(vc))
        sp['sel_k'].append(kv5(ks)); sp['sel_v'].append(kv5(vs))
        sp['win_k'].append(kv5(kw)[:, -n_buf:]); sp['win_v'].append(kv5(vw)[:, -n_buf:])
        sp['conv'].append(conv_st[:, 32 - (CONV_W - 1):])
        sp['shift'].append(rw.reshape(batch, seq, RW_COLS)[:, -1])
        sp['wkv'].append(s_fin)

        hs = _ffn(hs, vec(nr[0]), vec(nr[1]), wg[0], wu[0], wd[0])
        yq, kc, vc, ks, vs, kw, vw, gt, cvcg, rw = _proj(hs, vec(nr[2]), w_in_p, *rope_s)
        abk_pool = _compress_pool(pool_k, l, pe_k, wb_k).reshape(n_pool, page // CMP_STRIDE, 256)
        abv_pool = _compress_pool(pool_v, l, pe_v, wb_v).reshape(n_pool, page // CMP_STRIDE, 256)
        q8 = yq.reshape(nb, 2, A_HEADS, HEAD_DIM)
        q2 = jnp.concatenate([jnp.where((grp == 0)[None, None, :, None], q8, 0.0),
                              jnp.where((grp == 1)[None, None, :, None], q8, 0.0)], axis=-1)
        q2 = q2.reshape(nb, 2 * A_HEADS, LANE)
        r3 = lambda x: x.reshape(nb, 1, -1)
        o2 = _nsa_decode(page_table, q2, r3(gt), r3(ks), r3(vs), r3(kw), r3(vw), wink, winv, ovt_s, emat_s,
                         abk_pool, abv_pool, selk, selv, l)
        o_a = o2.reshape(nb, A_WIDTH)
        o_b, conv_new = _conv_sample(cvcg, state_conv, l, *conv_params)
        nk, wdec, bb, kmod, rp, v, vcc, vbb, gate = _rwkv_prep_sample(rw, state_shift, l, rw_params)
        o_s, s_new = _wkv_sample(nk, wdec, bb, kmod, rp, v, ones64, state_wkv, l)
        o_s = o_s.transpose(1, 0, 2).reshape(nb, C_WIDTH)
        hs = _merge(hs, o_a, o_b, o_s, vcc, vbb, gate, *merge_params)
        hs = _ffn(hs, vec(nr[4]), vec(nr[5]), wg[1], wu[1], wd[1])
        kv5s = lambda x: x.reshape(nb, 1, A_KV_HEADS, HEAD_DIM)
        ss['cmp_k'].append(kv5s(kc)); ss['cmp_v'].append(kv5s(vc))
        ss['sel_k'].append(kv5s(ks)); ss['sel_v'].append(kv5s(vs))
        ss['win_k'].append(jnp.concatenate([cache_win_k[l], kv5s(kw)], axis=1)[:, -win_buf:])
        ss['win_v'].append(jnp.concatenate([cache_win_v[l], kv5s(vw)], axis=1)[:, -win_buf:])
        ss['conv'].append(conv_new)
        ss['shift'].append(rw)
        ss['wkv'].append(s_new)

    keys = ('cmp_k', 'cmp_v', 'sel_k', 'sel_v', 'win_k', 'win_v', 'conv', 'shift', 'wkv')
    st = lambda d: tuple(jnp.stack(d[k]) for k in keys)
    return (hp.reshape(batch, seq, D_MODEL), hs.reshape(nb, 1, D_MODEL)) + st(sp) + st(ss)
```
